```python
import jax
import jax.numpy as jnp
from jax import lax
import numpy as np

D_MODEL = 2048
BATCH = 4
SEQ = 4096
DEPTH = 2

MIX_WIDTH = D_MODEL
MLSTM_WIDTH = MIX_WIDTH // 2
MLSTM_HEADS = 4
MLSTM_DV = MLSTM_WIDTH // MLSTM_HEADS
MLSTM_DK = MLSTM_DV // 2
MLSTM_CHUNK = 64
POOL_WIDTH = MIX_WIDTH - MLSTM_WIDTH
POOL_WINDOWS = (2, 4, 8, 16)
POOL_GROUP = POOL_WIDTH // len(POOL_WINDOWS)
EVEN_SIZES = (MLSTM_HEADS * MLSTM_DK, MLSTM_HEADS * MLSTM_DK, MLSTM_WIDTH, MLSTM_WIDTH, MLSTM_HEADS, MLSTM_HEADS, POOL_WIDTH)
EVEN_IN = sum(EVEN_SIZES)

GLA_HEADS = 4
GLA_DK_TOTAL = D_MODEL // 2
GLA_DV_TOTAL = D_MODEL
GLA_DK = GLA_DK_TOTAL // GLA_HEADS
GLA_DV = GLA_DV_TOTAL // GLA_HEADS
GLA_GATE_RANK = 16
GLA_TAU = 16.0
GLA_CHUNK = 32
ODD_SIZES = (GLA_DK_TOTAL, GLA_DK_TOTAL, GLA_DV_TOTAL, GLA_DV_TOTAL, GLA_GATE_RANK)
ODD_IN = sum(ODD_SIZES)

PEER_KEYS = 128
PEER_EXPERTS = PEER_KEYS * PEER_KEYS
PEER_HEADS = 8
PEER_QDIM = 256
PEER_HALF = PEER_QDIM // 2
PEER_TOPK = 16
PEER_BLOCK = 128

N_EVEN = (DEPTH + 1) // 2
N_ODD = DEPTH // 2
EPS = 1e-6

kernel_name = 'hybrid_mlstm_pool_gla_peer'


def _split(z, sizes):
    return jnp.split(z, np.cumsum(sizes)[:-1].tolist(), axis=-1)


def rmsnorm(x, g):
    xf = x.astype(jnp.float32)
    y = xf * lax.rsqrt(jnp.mean(xf * xf, axis=-1, keepdims=True) + EPS)
    return (y * g.astype(jnp.float32)).astype(x.dtype)


def headnorm(x, g):
    H, d = x.shape[-2:]
    y = x * lax.rsqrt(jnp.mean(x * x, axis=-1, keepdims=True) + EPS)
    return y * g.astype(jnp.float32).reshape(H, d)


def _to_chunks(a, L):
    nc = a.shape[2] // L
    a = a.reshape(a.shape[:2] + (nc, L) + a.shape[3:])
    return jnp.moveaxis(a, 2, 0)


def _from_chunks(a):
    a = jnp.moveaxis(a, 0, 2)
    return a.reshape(a.shape[:2] + (a.shape[2] * a.shape[3],) + a.shape[4:])


def mlstm_chunkwise(q, k, v, i_pre, f_pre):
    f32 = jnp.float32
    B, H, S, DK = q.shape
    DV = v.shape[-1]
    L = MLSTM_CHUNK
    q = q.astype(f32)
    k = k.astype(f32) * (DK ** -0.5)
    v = v.astype(f32)
    ig = i_pre.astype(f32)
    lf = jax.nn.log_sigmoid(f_pre.astype(f32))
    causal = jnp.tril(jnp.ones((L, L), dtype=bool))

    def step(carry, inp):
        C, n, m = carry
        q_, k_, v_, i_, f_ = inp
        b = jnp.cumsum(f_, axis=-1)
        dmat = jnp.where(causal, b[..., :, None] - b[..., None, :] + i_[..., None, :], -jnp.inf)
        inter = b + m[..., None]
        m_t = jnp.maximum(jnp.max(dmat, axis=-1), inter)
        a_inter = jnp.exp(inter - m_t)
        s = jnp.einsum('bhtd,bhsd->bhts', q_, k_) * jnp.exp(dmat - m_t[..., None])
        num = jnp.einsum('bhts,bhsv->bhtv', s, v_) + a_inter[..., None] * jnp.einsum('bhvd,bhtd->bhtv', C, q_)
        den = jnp.sum(s, axis=-1) + a_inter * jnp.einsum('bhd,bhtd->bht', n, q_)
        h = num / jnp.maximum(jnp.abs(den), jnp.exp(-m_t))[..., None]
        bL = b[..., -1]
        dec = bL[..., None] - b + i_
        m_new = jnp.maximum(bL + m, jnp.max(dec, axis=-1))
        wk = jnp.exp(dec - m_new[..., None])
        sc = jnp.exp(bL + m - m_new)
        C_new = sc[..., None, None] * C + jnp.einsum('bhs,bhsv,bhsd->bhvd', wk, v_, k_)
        n_new = sc[..., None] * n + jnp.einsum('bhs,bhsd->bhd', wk, k_)
        return (C_new, n_new, m_new), h

    init = (jnp.zeros((B, H, DV, DK), f32), jnp.zeros((B, H, DK), f32), jnp.zeros((B, H), f32))
    xs = (_to_chunks(q, L), _to_chunks(k, L), _to_chunks(v, L), _to_chunks(ig, L), _to_chunks(lf, L))
    _, h = lax.scan(step, init, xs)
    return _from_chunks(h)


def multiscale_pool(p, w_pool, pool_scale):
    f32 = jnp.float32
    S = p.shape[1]
    pf = p.astype(f32)
    cs = jnp.pad(jnp.cumsum(pf, axis=1), ((0, 0), (1, 0), (0, 0)))
    t = jnp.arange(S)
    outs = []
    for g, w in enumerate(POOL_WINDOWS):
        sl = slice(g * POOL_GROUP, (g + 1) * POOL_GROUP)
        c = cs[..., sl]
        lo = jnp.maximum(t + 1 - w, 0)
        cnt = jnp.minimum(t + 1, w).astype(f32)
        mixed = (c[:, 1:] - c[:, lo]) / cnt[:, None] - pf[..., sl]
        outs.append(jnp.einsum('bsc,cd->bsd', mixed, w_pool[g].astype(f32)))
    y = jnp.concatenate(outs, axis=-1) * pool_scale.astype(f32)
    return y.astype(p.dtype)


def gla_chunkwise(q, k, v, log_a):
    f32 = jnp.float32
    B, H, S, DK = q.shape
    DV = v.shape[-1]
    L = GLA_CHUNK
    q = q.astype(f32) * (DK ** -0.5)
    k = k.astype(f32)
    v = v.astype(f32)
    la = log_a.astype(f32)
    causal = jnp.tril(jnp.ones((L, L), dtype=bool))

    def step(St, inp):
        q_, k_, v_, a_ = inp
        b = jnp.cumsum(a_, axis=-2)
        rel = jnp.where(causal[..., None], b[..., :, None, :] - b[..., None, :, :], -jnp.inf)
        att = jnp.einsum('bhtd,bhsd,bhtsd->bhts', q_, k_, jnp.exp(rel))
        o = jnp.einsum('bhts,bhsv->bhtv', att, v_) + jnp.einsum('bhtd,bhdv->bhtv', q_ * jnp.exp(b), St)
        bL = b[..., -1:, :]
        S_new = jnp.exp(bL[..., 0, :])[..., None] * St + jnp.einsum('bhsd,bhsv->bhdv', k_ * jnp.exp(bL - b), v_)
        return S_new, o

    xs = (_to_chunks(q, L), _to_chunks(k, L), _to_chunks(v, L), _to_chunks(la, L))
    _, o = lax.scan(step, jnp.zeros((B, H, DK, DV), f32), xs)
    return _from_chunks(o)


def even_mixer(h, w_in, b_igate, b_fgate, head_gain, w_pool, pool_scale, w_out):
    B, S, _ = h.shape
    H = MLSTM_HEADS
    z = h @ w_in
    q, k, v, o, ig, fg, p = _split(z, EVEN_SIZES)

    def heads(a, d):
        return a.reshape(B, S, H, d).transpose(0, 2, 1, 3)

    hm = mlstm_chunkwise(heads(q, MLSTM_DK), heads(k, MLSTM_DK), heads(v, MLSTM_DV),
                         (ig + b_igate).transpose(0, 2, 1), (fg + b_fgate).transpose(0, 2, 1))
    hm = headnorm(hm.transpose(0, 2, 1, 3), head_gain).reshape(B, S, MLSTM_WIDTH)
    hm = hm * jax.nn.sigmoid(o.astype(jnp.float32))
    hp = multiscale_pool(p, w_pool, pool_scale)
    y = jnp.concatenate([hm.astype(h.dtype), hp], axis=-1)
    return y @ w_out


def odd_mixer(h, w_in, w_gate, b_gate, head_gain, w_out):
    B, S, _ = h.shape
    H = GLA_HEADS
    z = h @ w_in
    q, k, v, r, gl = _split(z, ODD_SIZES)
    log_a = jax.nn.log_sigmoid((gl @ w_gate + b_gate).astype(jnp.float32)) / GLA_TAU

    def heads(a, d):
        return a.reshape(B, S, H, d).transpose(0, 2, 1, 3)

    o = gla_chunkwise(heads(q, GLA_DK), heads(k, GLA_DK), heads(v, GLA_DV), heads(log_a, GLA_DK))
    o = headnorm(o.transpose(0, 2, 1, 3), head_gain).reshape(B, S, GLA_DV_TOTAL)
    y = (o * jax.nn.silu(r.astype(jnp.float32))).astype(h.dtype)
    return y @ w_out


def peer_ffn(h, w_q, sub_keys, u_tab, v_tab):
    f32 = jnp.float32
    B, S, D = h.shape
    hb = h.reshape((B * S) // PEER_BLOCK, PEER_BLOCK, D)
    keys = sub_keys.astype(f32)

    def block(xb):
        T = xb.shape[0]
        q = (xb @ w_q).reshape(T, PEER_HEADS, 2, PEER_HALF).astype(f32)
        s = jnp.einsum('thpc,hpnc->thpn', q, keys)
        sv, si = lax.top_k(s, PEER_TOPK)
        cand = (sv[:, :, 0, :, None] + sv[:, :, 1, None, :]).reshape(T, PEER_HEADS, PEER_TOPK * PEER_TOPK)
        cv, ci = lax.top_k(cand, PEER_TOPK)
        e1 = jnp.take_along_axis(si[:, :, 0], ci // PEER_TOPK, axis=-1)
        e2 = jnp.take_along_axis(si[:, :, 1], ci % PEER_TOPK, axis=-1)
        eid = e1 * PEER_KEYS + e2
        g = jax.nn.softmax(cv, axis=-1)
        u = jnp.take(u_tab, eid, axis=0)
        a = jnp.einsum('td,thkd->thk', xb, u).astype(f32)
        coef = (g * jax.nn.gelu(a, approximate=False)).astype(xb.dtype)
        vv = jnp.take(v_tab, eid, axis=0)
        return jnp.einsum('thk,thkd->td', coef, vv)

    y = lax.map(block, hb)
    return y.reshape(B, S, D)


def setup_inputs(seed: int = 0) -> dict:
    key = jax.random.key(seed)
    ks = jax.random.split(key, 21)
    f32 = jnp.float32

    def nrm(k, shape, std):
        return jax.random.normal(k, shape, f32) * std

    def gain(k, shape):
        return 1.0 + 0.02 * jax.random.normal(k, shape, f32)

    return {
        'x': jax.random.normal(ks[0], (BATCH, SEQ, D_MODEL), f32),
        'e_norm': gain(ks[1], (N_EVEN, D_MODEL)),
        'e_w_in': nrm(ks[2], (N_EVEN, D_MODEL, EVEN_IN), D_MODEL ** -0.5),
        'e_b_igate': nrm(ks[3], (N_EVEN, MLSTM_HEADS), 0.1),
        'e_b_fgate': 3.0 + nrm(ks[4], (N_EVEN, MLSTM_HEADS), 0.5),
        'e_head_gain': gain(ks[5], (N_EVEN, MLSTM_WIDTH)),
        'e_w_pool': nrm(ks[6], (N_EVEN, len(POOL_WINDOWS), POOL_GROUP, POOL_GROUP), POOL_GROUP ** -0.5),
        'e_pool_scale': gain(ks[7], (N_EVEN, POOL_WIDTH)),
        'e_w_out': nrm(ks[8], (N_EVEN, MIX_WIDTH, D_MODEL), MIX_WIDTH ** -0.5),
        'o_norm': gain(ks[9], (N_ODD, D_MODEL)),
        'o_w_in': nrm(ks[10], (N_ODD, D_MODEL, ODD_IN), D_MODEL ** -0.5),
        'o_w_gate': nrm(ks[11], (N_ODD, GLA_GATE_RANK, GLA_DK_TOTAL), GLA_GATE_RANK ** -0.5),
        'o_b_gate': nrm(ks[12], (N_ODD, GLA_DK_TOTAL), 0.1),
        'o_head_gain': gain(ks[13], (N_ODD, GLA_DV_TOTAL)),
        'o_w_out': nrm(ks[14], (N_ODD, GLA_DV_TOTAL, D_MODEL), GLA_DV_TOTAL ** -0.5),
        'f_norm': gain(ks[15], (DEPTH, D_MODEL)),
        'f_w_q': nrm(ks[16], (DEPTH, D_MODEL, PEER_HEADS * PEER_QDIM), D_MODEL ** -0.5),
        'f_sub_keys': nrm(ks[17], (DEPTH, PEER_HEADS, 2, PEER_KEYS, PEER_HALF), PEER_HALF ** -0.5),
        'f_u': nrm(ks[18], (DEPTH, PEER_EXPERTS, D_MODEL), D_MODEL ** -0.5),
        'f_v': nrm(ks[19], (DEPTH, PEER_EXPERTS, D_MODEL), PEER_HEADS ** -0.5),
        'final_norm': gain(ks[20], (D_MODEL,)),
    }


def reference(x, e_norm, e_w_in, e_b_igate, e_b_fgate, e_head_gain, e_w_pool, e_pool_scale, e_w_out,
              o_norm, o_w_in, o_w_gate, o_b_gate, o_head_gain, o_w_out,
              f_norm, f_w_q, f_sub_keys, f_u, f_v, final_norm):
    for layer in range(DEPTH):
        j = layer // 2
        if layer % 2 == 0:
            x = x + even_mixer(rmsnorm(x, e_norm[j]), e_w_in[j], e_b_igate[j], e_b_fgate[j], e_head_gain[j],
                               e_w_pool[j], e_pool_scale[j], e_w_out[j])
        else:
            x = x + odd_mixer(rmsnorm(x, o_norm[j]), o_w_in[j], o_w_gate[j], o_b_gate[j], o_head_gain[j], o_w_out[j])
        x = x + peer_ffn(rmsnorm(x, f_norm[layer]), f_w_q[layer], f_sub_keys[layer], f_u[layer], f_v[layer])
    return rmsnorm(x, final_norm)
```

```python
import functools

import jax
import jax.numpy as jnp
from jax import lax
from jax.experimental import pallas as pl
from jax.experimental.pallas import tpu as pltpu

F32 = jnp.float32
BF16 = jnp.bfloat16
EPS = 1e-6
LANE = 128
VMEM_LIMIT = 56 * 1024 * 1024

MLSTM_HEADS = 4
MLSTM_CHUNK = 64
POOL_WINDOWS = (2, 4, 8, 16)
POOL_HALO = 16
GLA_HEADS = 4
GLA_CHUNK = 32
GLA_TAU = 16.0
PEER_TOPK = 16
NEG_INF = float("-inf")

NT_DIMS = (((1,), (1,)), ((), ()))
TN_DIMS = (((0,), (0,)), ((), ()))


def _params(*sem):
    return pltpu.CompilerParams(dimension_semantics=sem, vmem_limit_bytes=VMEM_LIMIT)


def _log_sigmoid(x):
    return jnp.minimum(x, 0.0) - jnp.log1p(jnp.exp(-jnp.abs(x)))


def _sigmoid(x):
    return 1.0 / (1.0 + jnp.exp(-x))


def _dot(a, b, dims=None):
    if dims is None:
        return jnp.dot(a, b, preferred_element_type=F32)
    return lax.dot_general(a, b, dims, preferred_element_type=F32)


def _cumsum_rows(x, tri):
    hi = x.astype(BF16)
    lo = (x - hi.astype(F32)).astype(BF16)
    return _dot(tri, hi) + _dot(tri, lo)


def _tril(n, dtype):
    r = lax.broadcasted_iota(jnp.int32, (n, n), 0)
    c = lax.broadcasted_iota(jnp.int32, (n, n), 1)
    return (c <= r).astype(dtype)


def _rms_kernel(x_ref, g_ref, o_ref, *, transpose):
    x = x_ref[...]
    y = x * lax.rsqrt(jnp.mean(x * x, axis=-1, keepdims=True) + EPS) * g_ref[...]
    if transpose:
        y = y.T
    o_ref[...] = y.astype(o_ref.dtype)


def rmsnorm(x, g, out_dtype, transpose=False):
    T, D = x.shape
    tm = min(T, 512)
    if transpose:
        out_shape, out_spec = (D, T), pl.BlockSpec((D, tm), lambda i: (0, i))
    else:
        out_shape, out_spec = (T, D), pl.BlockSpec((tm, D), lambda i: (i, 0))
    return pl.pallas_call(
        functools.partial(_rms_kernel, transpose=transpose),
        grid=(T // tm,),
        in_specs=[pl.BlockSpec((tm, D), lambda i: (i, 0)), pl.BlockSpec((1, D), lambda i: (0, 0))],
        out_specs=out_spec,
        out_shape=jax.ShapeDtypeStruct(out_shape, out_dtype),
        compiler_params=_params("parallel"),
        name="rmsnorm_t" if transpose else "rmsnorm",
    )(x, g.reshape(1, D).astype(F32))


def _mm_kernel(a_ref, b_ref, o_ref):
    o_ref[...] = _dot(a_ref[...], b_ref[...]).astype(o_ref.dtype)


def _mm_res_kernel(a_ref, b_ref, r_ref, o_ref):
    o_ref[...] = (_dot(a_ref[...], b_ref[...]) + r_ref[...]).astype(o_ref.dtype)


def matmul(a, b, out_dtype, tn, residual=None, name="matmul"):
    M, K = a.shape
    N = b.shape[1]
    tm = min(M, 1024)
    tn = min(N, tn)
    in_specs = [pl.BlockSpec((tm, K), lambda i, j: (i, 0)), pl.BlockSpec((K, tn), lambda i, j: (0, j))]
    args = [a, b]
    body = _mm_kernel
    if residual is not None:
        in_specs.append(pl.BlockSpec((tm, tn), lambda i, j: (i, j)))
        args.append(residual)
        body = _mm_res_kernel
    return pl.pallas_call(
        body,
        grid=(M // tm, N // tn),
        in_specs=in_specs,
        out_specs=pl.BlockSpec((tm, tn), lambda i, j: (i, j)),
        out_shape=jax.ShapeDtypeStruct((M, N), out_dtype),
        compiler_params=_params("parallel", "parallel"),
        name=name,
    )(*args)


def _mlstm_kernel(q_ref, k_ref, v_ref, o_ref, g_ref, gb_ref, gain_ref, out_ref, cn_ref, m_ref,
                  *, heads, dk, dv, chunk, nchunks):
    L = chunk

    @pl.when(pl.program_id(1) == 0)
    def _():
        cn_ref[...] = jnp.zeros_like(cn_ref)
        m_ref[...] = jnp.zeros_like(m_ref)

    tri_b = _tril(L, BF16)
    causal = _tril(L, jnp.int32) > 0
    ones_col = (lax.broadcasted_iota(jnp.int32, (L, LANE), 1) == 0).astype(F32)
    kscale = dk ** -0.5

    def chunk_body(c, carry):
        r0 = pl.multiple_of(c * L, L)
        rows = pl.ds(r0, L)
        g = g_ref[rows, :] + gb_ref[...]
        bc = _cumsum_rows(_log_sigmoid(g), tri_b)
        g_t = g.T
        bc_t = bc.T
        for h in range(heads):
            i_col = g[:, h:h + 1]
            i_row = g_t[h:h + 1, :]
            b_col = bc[:, heads + h:heads + h + 1]
            b_row = bc_t[heads + h:heads + h + 1, :]
            m_prev = m_ref[h:h + 1, 0:1]
            qb = q_ref[rows, h * dk:(h + 1) * dk].astype(BF16)
            kb = (k_ref[rows, h * dk:(h + 1) * dk] * kscale).astype(BF16)
            v_aug = jnp.concatenate([v_ref[rows, h * dv:(h + 1) * dv], ones_col], axis=1)
            cn = cn_ref[h]

            dmat = jnp.where(causal, b_col - b_row + i_row, NEG_INF)
            inter = b_col + m_prev
            m_t = jnp.maximum(jnp.max(dmat, axis=-1, keepdims=True), inter)
            a_inter = jnp.exp(inter - m_t)
            s = _dot(qb, kb, NT_DIMS) * jnp.exp(dmat - m_t)
            hn = _dot(s.astype(BF16), v_aug.astype(BF16)) + a_inter * _dot(qb, cn.astype(BF16), NT_DIMS)
            num = hn[:, :dv]
            den = hn[:, dv:dv + 1]
            hh = num / jnp.maximum(jnp.abs(den), jnp.exp(-m_t))

            b_last = b_col[L - 1:L, :]
            dec = b_last - b_col + i_col
            m_new = jnp.maximum(b_last + m_prev, jnp.max(dec, axis=0, keepdims=True))
            wk = jnp.exp(dec - m_new)
            sc = jnp.exp(b_last + m_prev - m_new)
            cn_ref[h] = sc * cn + _dot((wk * v_aug).astype(BF16), kb, TN_DIMS)
            m_ref[h:h + 1, :] = jnp.broadcast_to(m_new, (1, LANE))

            y = hh * lax.rsqrt(jnp.mean(hh * hh, axis=-1, keepdims=True) + EPS) * gain_ref[:, h * dv:(h + 1) * dv]
            y = y * _sigmoid(o_ref[rows, h * dv:(h + 1) * dv])
            out_ref[rows, h * dv:(h + 1) * dv] = y.astype(out_ref.dtype)
        return carry

    lax.fori_loop(0, nchunks, chunk_body, 0)


def mlstm_block(z, gate_bias, head_gain, batch, seq, q_off, k_off, v_off, o_off, g_off):
    T = z.shape[0]
    H, L = MLSTM_HEADS, MLSTM_CHUNK
    width = head_gain.shape[0]
    dv = width // H
    dk = dv // 2
    ts = min(seq, 512)
    nblk = seq // ts

    def col(off, w):
        assert off % w == 0
        return pl.BlockSpec((ts, w), lambda b, s: (b * nblk + s, off // w))

    return pl.pallas_call(
        functools.partial(_mlstm_kernel, heads=H, dk=dk, dv=dv, chunk=L, nchunks=ts // L),
        grid=(batch, nblk),
        in_specs=[col(q_off, H * dk), col(k_off, H * dk), col(v_off, width), col(o_off, width), col(g_off, LANE),
                  pl.BlockSpec((1, LANE), lambda b, s: (0, 0)), pl.BlockSpec((1, width), lambda b, s: (0, 0))],
        out_specs=pl.BlockSpec((ts, width), lambda b, s: (b * nblk + s, 0)),
        out_shape=jax.ShapeDtypeStruct((T, width), BF16),
        scratch_shapes=[pltpu.VMEM((H, dv + LANE, dk), F32), pltpu.VMEM((8, LANE), F32)],
        compiler_params=_params("parallel", "arbitrary"),
        name="mlstm",
    )(z, z, z, z, z, gate_bias, head_gain.reshape(1, width).astype(F32))


def _pool_kernel(p_ref, w_ref, s_ref, out_ref, halo_ref, *, group, ts):
    si = pl.program_id(1)

    @pl.when(si == 0)
    def _():
        halo_ref[...] = jnp.zeros_like(halo_ref)

    p = p_ref[...]
    buf = jnp.concatenate([halo_ref[...], p], axis=0)
    t_abs = si * ts + lax.broadcasted_iota(jnp.int32, (ts, 1), 0)
    win = buf
    span = 1
    for g, w in enumerate(POOL_WINDOWS):
        while span < w:
            win = win + pltpu.roll(win, span, 0)
            span *= 2
        sl = slice(g * group, (g + 1) * group)
        cnt = jnp.minimum(t_abs + 1, w).astype(F32)
        mixed = win[POOL_HALO:, sl] / cnt - p[:, sl]
        y = _dot(mixed.astype(BF16), w_ref[g]) * s_ref[:, sl]
        out_ref[:, sl] = y.astype(out_ref.dtype)
    halo_ref[...] = p[ts - POOL_HALO:, :]


def pool_block(z, w_pool, pool_scale, batch, seq, p_off):
    T = z.shape[0]
    ng, group, _ = w_pool.shape
    width = ng * group
    ts = min(seq, 512)
    nblk = seq // ts
    assert p_off % width == 0
    return pl.pallas_call(
        functools.partial(_pool_kernel, group=group, ts=ts),
        grid=(batch, nblk),
        in_specs=[pl.BlockSpec((ts, width), lambda b, s: (b * nblk + s, p_off // width)),
                  pl.BlockSpec((ng, group, group), lambda b, s: (0, 0, 0)),
                  pl.BlockSpec((1, width), lambda b, s: (0, 0))],
        out_specs=pl.BlockSpec((ts, width), lambda b, s: (b * nblk + s, 0)),
        out_shape=jax.ShapeDtypeStruct((T, width), BF16),
        scratch_shapes=[pltpu.VMEM((POOL_HALO, width), F32)],
        compiler_params=_params("parallel", "arbitrary"),
        name="pool",
    )(z, w_pool.astype(BF16), pool_scale.reshape(1, width).astype(F32))


def _gla_kernel(q_ref, k_ref, v_ref, r_ref, gl_ref, wg_ref, bg_ref, gain_ref, out_ref, st_ref,
                *, heads, dk, dv, chunk, nchunks):
    L = chunk

    @pl.when(pl.program_id(1) == 0)
    def _():
        st_ref[...] = jnp.zeros_like(st_ref)

    tri_b = _tril(L, BF16)
    row_id = lax.broadcasted_iota(jnp.int32, (L, 1), 0)
    lane_id = lax.broadcasted_iota(jnp.int32, (L, L), 1)
    qscale = dk ** -0.5

    def chunk_body(c, carry):
        r0 = pl.multiple_of(c * L, L)
        rows = pl.ds(r0, L)
        la = _log_sigmoid(_dot(gl_ref[rows, :].astype(BF16), wg_ref[...]) + bg_ref[...]) / GLA_TAU
        b_all = _cumsum_rows(la, tri_b)
        for h in range(heads):
            b = b_all[:, h * dk:(h + 1) * dk]
            q = q_ref[rows, h * dk:(h + 1) * dk] * qscale
            k = k_ref[rows, h * dk:(h + 1) * dk]
            vb = v_ref[rows, h * dv:(h + 1) * dv].astype(BF16)
            st = st_ref[h]

            att = jnp.zeros((L, L), F32)
            for s in range(L):
                rel = jnp.where(row_id >= s, b - b[s:s + 1, :], NEG_INF)
                col = jnp.sum(q * k[s:s + 1, :] * jnp.exp(rel), axis=-1, keepdims=True)
                att = jnp.where(lane_id == s, col, att)
            b_last = b[L - 1:L, :]
            o = _dot(att.astype(BF16), vb) + _dot((q * jnp.exp(b)).astype(BF16), st.astype(BF16), NT_DIMS)
            kdec = (k * jnp.exp(b_last - b)).astype(BF16)
            st_ref[h] = jnp.exp(b_last) * st + _dot(vb, kdec, TN_DIMS)

            y = o * lax.rsqrt(jnp.mean(o * o, axis=-1, keepdims=True) + EPS) * gain_ref[:, h * dv:(h + 1) * dv]
            r = r_ref[rows, h * dv:(h + 1) * dv]
            y = y * (r * _sigmoid(r))
            out_ref[rows, h * dv:(h + 1) * dv] = y.astype(out_ref.dtype)
        return carry

    lax.fori_loop(0, nchunks, chunk_body, 0)


def gla_block(z, w_gate, b_gate, head_gain, batch, seq, q_off, k_off, v_off, r_off, g_off):
    T = z.shape[0]
    H, L = GLA_HEADS, GLA_CHUNK
    dkt = w_gate.shape[1]
    dvt = head_gain.shape[0]
    dk, dv = dkt // H, dvt // H
    ts = min(seq, 256)
    nblk = seq // ts

    def col(off, w):
        assert off % w == 0
        return pl.BlockSpec((ts, w), lambda b, s: (b * nblk + s, off // w))

    return pl.pallas_call(
        functools.partial(_gla_kernel, heads=H, dk=dk, dv=dv, chunk=L, nchunks=ts // L),
        grid=(batch, nblk),
        in_specs=[col(q_off, dkt), col(k_off, dkt), col(v_off, dvt), col(r_off, dvt), col(g_off, LANE),
                  pl.BlockSpec((LANE, dkt), lambda b, s: (0, 0)),
                  pl.BlockSpec((1, dkt), lambda b, s: (0, 0)),
                  pl.BlockSpec((1, dvt), lambda b, s: (0, 0))],
        out_specs=pl.BlockSpec((ts, dvt), lambda b, s: (b * nblk + s, 0)),
        out_shape=jax.ShapeDtypeStruct((T, dvt), BF16),
        scratch_shapes=[pltpu.VMEM((H, dv, dk), F32)],
        compiler_params=_params("parallel", "arbitrary"),
        name="gla",
    )(z, z, z, z, z, w_gate, b_gate.reshape(1, dkt).astype(F32), head_gain.reshape(1, dvt).astype(F32))


def _extract_topk(vals, k):
    rank = jnp.full(vals.shape, float(k), F32)
    tops = []
    for i in range(k):
        mx = jnp.max(vals, axis=0, keepdims=True)
        hit = vals == mx
        rank = jnp.where(hit, float(i), rank)
        vals = jnp.where(hit, NEG_INF, vals)
        tops.append(mx)
    return jnp.concatenate(tops, axis=0), rank


def _peer_topk_kernel(qt_ref, keys_ref, cnt1_ref, p1_ref, r2_ref, p2_ref, *, heads, half, topk):
    K = topk
    for h in range(heads):
        s1 = _dot(keys_ref[h, 0], qt_ref[(2 * h) * half:(2 * h + 1) * half, :])
        s2 = _dot(keys_ref[h, 1], qt_ref[(2 * h + 1) * half:(2 * h + 2) * half, :])
        sv1, rank1 = _extract_topk(s1, K)
        sv2, rank2 = _extract_topk(s2, K)
        cand = [sv1[a:a + 1, :] + sv2 for a in range(K)]
        work = jnp.concatenate(cand, axis=0)
        tau = None
        for _ in range(K):
            tau = jnp.max(work, axis=0, keepdims=True)
            work = jnp.where(work == tau, NEG_INF, work)
        cmax = sv1[0:1, :] + sv2[0:1, :]
        zsum = jnp.zeros_like(cmax)
        cnt1 = jnp.zeros_like(s1)
        for a in range(K):
            sel = cand[a] >= tau
            zsum = zsum + jnp.sum(jnp.where(sel, jnp.exp(cand[a] - cmax), 0.0), axis=0, keepdims=True)
            cnt_a = jnp.sum(sel.astype(F32), axis=0, keepdims=True)
            cnt1 = jnp.where(rank1 == float(a), cnt_a, cnt1)
        cnt1_ref[h] = cnt1
        p1_ref[h] = jnp.exp(s1 - sv1[0:1, :]) / zsum
        r2_ref[h] = rank2
        p2_ref[h] = jnp.exp(s2 - sv2[0:1, :])


def peer_topk(qt, keys):
    heads, _, n_keys, half = keys.shape
    T = qt.shape[1]
    tt = min(T, 256)
    out = jax.ShapeDtypeStruct((heads, n_keys, T), F32)
    ospec = pl.BlockSpec((heads, n_keys, tt), lambda i: (0, 0, i))
    return pl.pallas_call(
        functools.partial(_peer_topk_kernel, heads=heads, half=half, topk=PEER_TOPK),
        grid=(T // tt,),
        in_specs=[pl.BlockSpec((heads * 2 * half, tt), lambda i: (0, i)),
                  pl.BlockSpec((heads, 2, n_keys, half), lambda i: (0, 0, 0, 0))],
        out_specs=[ospec] * 4,
        out_shape=[out] * 4,
        compiler_params=_params("parallel"),
        name="peer_topk",
    )(qt, keys)


def _peer_dense_kernel(xt_ref, u_ref, v_ref, cnt1_ref, p1_ref, r2_ref, p2_ref, res_ref, out_ref, acc_ref,
                       *, heads, n_keys, rows_per_step):
    j = pl.program_id(1)

    @pl.when(j == 0)
    def _():
        acc_ref[...] = jnp.zeros_like(acc_ref)

    a_t = _dot(u_ref[...], xt_ref[...])
    coef = []
    for r in range(rows_per_step):
        e1 = j * rows_per_step + r
        a_r = a_t[r * n_keys:(r + 1) * n_keys, :]
        gate = jnp.zeros_like(a_r)
        for h in range(heads):
            cnt = cnt1_ref[h, pl.ds(e1, 1), :]
            p1 = p1_ref[h, pl.ds(e1, 1), :]
            gate = gate + jnp.where(r2_ref[h] < cnt, p2_ref[h] * p1, 0.0)
        gelu = 0.5 * a_r * (1.0 + lax.erf(a_r * (2.0 ** -0.5)))
        coef.append((gate * gelu).astype(BF16))
    coef = jnp.concatenate(coef, axis=0)
    acc_ref[...] += _dot(coef, v_ref[...], TN_DIMS)

    @pl.when(j == pl.num_programs(1) - 1)
    def _():
        out_ref[...] = res_ref[...] + acc_ref[...]


def peer_dense(xt, u, v, cnt1, p1, r2, p2, residual):
    D, T = xt.shape
    heads, n_keys, _ = cnt1.shape
    E = u.shape[0]
    rows_per_step = 2
    eb = rows_per_step * n_keys
    tm = min(T, 512)
    gspec = pl.BlockSpec((heads, n_keys, tm), lambda i, j: (0, 0, i))
    return pl.pallas_call(
        functools.partial(_peer_dense_kernel, heads=heads, n_keys=n_keys, rows_per_step=rows_per_step),
        grid=(T // tm, E // eb),
        in_specs=[pl.BlockSpec((D, tm), lambda i, j: (0, i)),
                  pl.BlockSpec((eb, D), lambda i, j: (j, 0)),
                  pl.BlockSpec((eb, D), lambda i, j: (j, 0)),
                  gspec, gspec, gspec, gspec,
                  pl.BlockSpec((tm, D), lambda i, j: (i, 0))],
        out_specs=pl.BlockSpec((tm, D), lambda i, j: (i, 0)),
        out_shape=jax.ShapeDtypeStruct((T, D), F32),
        scratch_shapes=[pltpu.VMEM((tm, D), F32)],
        compiler_params=_params("parallel", "arbitrary"),
        name="peer_dense",
    )(xt, u, v, cnt1, p1, r2, p2, residual)


def peer_layer(x, norm_g, w_q, sub_keys, u_tab, v_tab):
    ht = rmsnorm(x, norm_g, BF16, transpose=True)
    qt = matmul(w_q.T.astype(BF16), ht, BF16, tn=512, name="peer_q")
    cnt1, p1, r2, p2 = peer_topk(qt, sub_keys.astype(BF16))
    return peer_dense(ht, u_tab.astype(BF16), v_tab.astype(BF16), cnt1, p1, r2, p2, x)


def _pad_cols(w, n):
    return jnp.pad(w, ((0, 0), (0, n - w.shape[1])))


def even_layer(x, batch, seq, norm_g, w_in, b_igate, b_fgate, head_gain, w_pool, pool_scale, w_out):
    H = MLSTM_HEADS
    width = head_gain.shape[0]
    qk = width // 2
    pool_w = pool_scale.shape[0]
    n_main = 2 * qk + 2 * width
    h = rmsnorm(x, norm_g, BF16)
    w_cat = jnp.concatenate([w_in[:, :n_main], w_in[:, n_main + 2 * H:], _pad_cols(w_in[:, n_main:n_main + 2 * H], LANE)],
                            axis=1).astype(BF16)
    z = matmul(h, w_cat, F32, tn=384, name="even_in")
    gate_bias = jnp.pad(jnp.concatenate([b_igate, b_fgate]), (0, LANE - 2 * H)).reshape(1, LANE).astype(F32)
    hm = mlstm_block(z, gate_bias, head_gain, batch, seq, 0, qk, 2 * qk, 2 * qk + width, n_main + pool_w)
    hp = pool_block(z, w_pool, pool_scale, batch, seq, n_main)
    y = jnp.concatenate([hm, hp], axis=1)
    return matmul(y, w_out.astype(BF16), F32, tn=512, residual=x, name="even_out")


def odd_layer(x, batch, seq, norm_g, w_in, w_gate, b_gate, head_gain, w_out):
    rank, dkt = w_gate.shape
    dvt = head_gain.shape[0]
    n_main = 2 * dkt + 2 * dvt
    h = rmsnorm(x, norm_g, BF16)
    w_cat = jnp.concatenate([w_in[:, :n_main], _pad_cols(w_in[:, n_main:], LANE)], axis=1).astype(BF16)
    z = matmul(h, w_cat, F32, tn=896, name="odd_in")
    wg = jnp.pad(w_gate, ((0, LANE - rank), (0, 0))).astype(BF16)
    y = gla_block(z, wg, b_gate, head_gain, batch, seq, 0, dkt, 2 * dkt, 2 * dkt + dvt, n_main)
    return matmul(y, w_out.astype(BF16), F32, tn=512, residual=x, name="odd_out")


def kernel(x, e_norm, e_w_in, e_b_igate, e_b_fgate, e_head_gain, e_w_pool, e_pool_scale, e_w_out, o_norm, o_w_in, o_w_gate, o_b_gate, o_head_gain, o_w_out, f_norm, f_w_q, f_sub_keys, f_u, f_v, final_norm):
    B, S, D = x.shape
    depth = f_norm.shape[0]
    h = x.reshape(B * S, D)
    for layer in range(depth):
        j = layer // 2
        if layer % 2 == 0:
            h = even_layer(h, B, S, e_norm[j], e_w_in[j], e_b_igate[j], e_b_fgate[j], e_head_gain[j],
                           e_w_pool[j], e_pool_scale[j], e_w_out[j])
        else:
            h = odd_layer(h, B, S, o_norm[j], o_w_in[j], o_w_gate[j], o_b_gate[j], o_head_gain[j], o_w_out[j])
        h = peer_layer(h, f_norm[layer], f_w_q[layer], f_sub_keys[layer], f_u[layer], f_v[layer])
    return rmsnorm(h, final_norm, F32).reshape(B, S, D)
```

```python
import functools

import jax
import jax.numpy as jnp
from jax import lax
from jax.experimental import pallas as pl
from jax.experimental.pallas import tpu as pltpu

F32 = jnp.float32
BF16 = jnp.bfloat16
EPS = 1e-6
LANE = 128
SUBLANE = 8
VMEM_LIMIT = 56 * 1024 * 1024

MLSTM_HEADS = 4
MLSTM_CHUNK = 64
POOL_WINDOWS = (2, 4, 8, 16)
POOL_HALO = 16
GLA_HEADS = 4
GLA_CHUNK = 32
GLA_TAU = 16.0
PEER_TOPK = 16
NEG_INF = float("-inf")

NT_DIMS = (((1,), (1,)), ((), ()))
TN_DIMS = (((0,), (0,)), ((), ()))


def _params(*sem):
    return pltpu.CompilerParams(dimension_semantics=sem, vmem_limit_bytes=VMEM_LIMIT)


def _log_sigmoid(x):
    return jnp.minimum(x, 0.0) - jnp.log1p(jnp.exp(-jnp.abs(x)))


def _sigmoid(x):
    return 1.0 / (1.0 + jnp.exp(-x))


def _dot(a, b, dims=None):
    if dims is None:
        return jnp.dot(a, b, preferred_element_type=F32)
    return lax.dot_general(a, b, dims, preferred_element_type=F32)


def _cumsum_rows(x, tri):
    hi = x.astype(BF16)
    lo = (x - hi.astype(F32)).astype(BF16)
    return _dot(tri, hi) + _dot(tri, lo)


def _tril(n, dtype):
    r = lax.broadcasted_iota(jnp.int32, (n, n), 0)
    c = lax.broadcasted_iota(jnp.int32, (n, n), 1)
    return (c <= r).astype(dtype)


def _rms_kernel(x_ref, g_ref, o_ref, *, transpose):
    x = x_ref[...]
    y = x * lax.rsqrt(jnp.mean(x * x, axis=-1, keepdims=True) + EPS) * g_ref[...]
    if transpose:
        y = y.T
    o_ref[...] = y.astype(o_ref.dtype)


def rmsnorm(x, g, out_dtype, transpose=False):
    T, D = x.shape
    tm = min(T, 512)
    if transpose:
        out_shape, out_spec = (D, T), pl.BlockSpec((D, tm), lambda i: (0, i))
    else:
        out_shape, out_spec = (T, D), pl.BlockSpec((tm, D), lambda i: (i, 0))
    return pl.pallas_call(
        functools.partial(_rms_kernel, transpose=transpose),
        grid=(T // tm,),
        in_specs=[pl.BlockSpec((tm, D), lambda i: (i, 0)), pl.BlockSpec((1, D), lambda i: (0, 0))],
        out_specs=out_spec,
        out_shape=jax.ShapeDtypeStruct(out_shape, out_dtype),
        compiler_params=_params("parallel"),
        name="rmsnorm_t" if transpose else "rmsnorm",
    )(x, g.reshape(1, D).astype(F32))


def _mm_kernel(a_ref, b_ref, o_ref):
    o_ref[...] = _dot(a_ref[...], b_ref[...]).astype(o_ref.dtype)


def _mm_res_kernel(a_ref, b_ref, r_ref, o_ref):
    o_ref[...] = (_dot(a_ref[...], b_ref[...]) + r_ref[...]).astype(o_ref.dtype)


def matmul(a, b, out_dtype, tn, residual=None, name="matmul"):
    M, K = a.shape
    N = b.shape[1]
    tm = min(M, 1024)
    tn = min(N, tn)
    in_specs = [pl.BlockSpec((tm, K), lambda i, j: (i, 0)), pl.BlockSpec((K, tn), lambda i, j: (0, j))]
    args = [a, b]
    body = _mm_kernel
    if residual is not None:
        in_specs.append(pl.BlockSpec((tm, tn), lambda i, j: (i, j)))
        args.append(residual)
        body = _mm_res_kernel
    return pl.pallas_call(
        body,
        grid=(M // tm, N // tn),
        in_specs=in_specs,
        out_specs=pl.BlockSpec((tm, tn), lambda i, j: (i, j)),
        out_shape=jax.ShapeDtypeStruct((M, N), out_dtype),
        compiler_params=_params("parallel", "parallel"),
        name=name,
    )(*args)


def _mlstm_kernel(q_ref, k_ref, v_ref, o_ref, g_ref, gb_ref, gain_ref, out_ref, cn_ref, m_ref,
                  *, heads, dk, dv, chunk, nchunks):
    L = chunk

    @pl.when(pl.program_id(1) == 0)
    def _():
        cn_ref[...] = jnp.zeros_like(cn_ref)
        m_ref[...] = jnp.zeros_like(m_ref)

    tri_b = _tril(L, BF16)
    causal = _tril(L, jnp.int32) > 0
    ones_col = (lax.broadcasted_iota(jnp.int32, (L, LANE), 1) == 0).astype(F32)
    kscale = dk ** -0.5

    def chunk_body(c, carry):
        r0 = pl.multiple_of(c * L, L)
        rows = pl.ds(r0, L)
        g = g_ref[rows, :] + gb_ref[...]
        bc = _cumsum_rows(_log_sigmoid(g), tri_b)
        g_t = g.T
        bc_t = bc.T
        for h in range(heads):
            i_col = g[:, h:h + 1]
            i_row = g_t[h:h + 1, :]
            b_col = bc[:, heads + h:heads + h + 1]
            b_row = bc_t[heads + h:heads + h + 1, :]
            m_prev = m_ref[h:h + 1, 0:1]
            qb = q_ref[rows, h * dk:(h + 1) * dk].astype(BF16)
            kb = (k_ref[rows, h * dk:(h + 1) * dk] * kscale).astype(BF16)
            v_aug = jnp.concatenate([v_ref[rows, h * dv:(h + 1) * dv], ones_col], axis=1)
            cn = cn_ref[h]

            dmat = jnp.where(causal, b_col - b_row + i_row, NEG_INF)
            inter = b_col + m_prev
            m_t = jnp.maximum(jnp.max(dmat, axis=-1, keepdims=True), inter)
            a_inter = jnp.exp(inter - m_t)
            s = _dot(qb, kb, NT_DIMS) * jnp.exp(dmat - m_t)
            hn = _dot(s.astype(BF16), v_aug.astype(BF16)) + a_inter * _dot(qb, cn.astype(BF16), NT_DIMS)
            num = hn[:, :dv]
            den = hn[:, dv:dv + 1]
            hh = num / jnp.maximum(jnp.abs(den), jnp.exp(-m_t))

            b_last = b_col[L - 1:L, :]
            dec = b_last - b_col + i_col
            m_new = jnp.maximum(b_last + m_prev, jnp.max(dec, axis=0, keepdims=True))
            wk = jnp.exp(dec - m_new)
            sc = jnp.exp(b_last + m_prev - m_new)
            cn_ref[h] = sc * cn + _dot((wk * v_aug).astype(BF16), kb, TN_DIMS)
            m_ref[h:h + 1, :] = jnp.broadcast_to(m_new, (1, LANE))

            y = hh * lax.rsqrt(jnp.mean(hh * hh, axis=-1, keepdims=True) + EPS) * gain_ref[:, h * dv:(h + 1) * dv]
            y = y * _sigmoid(o_ref[rows, h * dv:(h + 1) * dv])
            out_ref[rows, h * dv:(h + 1) * dv] = y.astype(out_ref.dtype)
        return carry

    lax.fori_loop(0, nchunks, chunk_body, 0)


def mlstm_block(z, gate_bias, head_gain, batch, seq, q_off, k_off, v_off, o_off, g_off):
    T = z.shape[0]
    H, L = MLSTM_HEADS, MLSTM_CHUNK
    width = head_gain.shape[0]
    dv = width // H
    dk = dv // 2
    ts = min(seq, 512)
    nblk = seq // ts

    def col(off, w):
        assert off % w == 0
        return pl.BlockSpec((ts, w), lambda b, s: (b * nblk + s, off // w))

    return pl.pallas_call(
        functools.partial(_mlstm_kernel, heads=H, dk=dk, dv=dv, chunk=L, nchunks=ts // L),
        grid=(batch, nblk),
        in_specs=[col(q_off, H * dk), col(k_off, H * dk), col(v_off, width), col(o_off, width), col(g_off, LANE),
                  pl.BlockSpec((1, LANE), lambda b, s: (0, 0)), pl.BlockSpec((1, width), lambda b, s: (0, 0))],
        out_specs=pl.BlockSpec((ts, width), lambda b, s: (b * nblk + s, 0)),
        out_shape=jax.ShapeDtypeStruct((T, width), BF16),
        scratch_shapes=[pltpu.VMEM((H, dv + LANE, dk), F32), pltpu.VMEM((8, LANE), F32)],
        compiler_params=_params("parallel", "arbitrary"),
        name="mlstm",
    )(z, z, z, z, z, gate_bias, head_gain.reshape(1, width).astype(F32))


def _pool_kernel(p_ref, w_ref, s_ref, out_ref, halo_ref, *, group, ts):
    si = pl.program_id(1)

    @pl.when(si == 0)
    def _():
        halo_ref[...] = jnp.zeros_like(halo_ref)

    p = p_ref[...]
    buf = jnp.concatenate([halo_ref[...], p], axis=0)
    t_abs = si * ts + lax.broadcasted_iota(jnp.int32, (ts, 1), 0)
    win = buf
    span = 1
    for g, w in enumerate(POOL_WINDOWS):
        while span < w:
            win = win + pltpu.roll(win, span, 0)
            span *= 2
        sl = slice(g * group, (g + 1) * group)
        cnt = jnp.minimum(t_abs + 1, w).astype(F32)
        mixed = win[POOL_HALO:, sl] / cnt - p[:, sl]
        y = _dot(mixed.astype(BF16), w_ref[g]) * s_ref[:, sl]
        out_ref[:, sl] = y.astype(out_ref.dtype)
    halo_ref[...] = p[ts - POOL_HALO:, :]


def pool_block(z, w_pool, pool_scale, batch, seq, p_off):
    T = z.shape[0]
    ng, group, _ = w_pool.shape
    width = ng * group
    ts = min(seq, 512)
    nblk = seq // ts
    assert p_off % width == 0
    return pl.pallas_call(
        functools.partial(_pool_kernel, group=group, ts=ts),
        grid=(batch, nblk),
        in_specs=[pl.BlockSpec((ts, width), lambda b, s: (b * nblk + s, p_off // width)),
                  pl.BlockSpec((ng, group, group), lambda b, s: (0, 0, 0)),
                  pl.BlockSpec((1, width), lambda b, s: (0, 0))],
        out_specs=pl.BlockSpec((ts, width), lambda b, s: (b * nblk + s, 0)),
        out_shape=jax.ShapeDtypeStruct((T, width), BF16),
        scratch_shapes=[pltpu.VMEM((POOL_HALO, width), F32)],
        compiler_params=_params("parallel", "arbitrary"),
        name="pool",
    )(z, w_pool.astype(BF16), pool_scale.reshape(1, width).astype(F32))


def _gla_kernel(q_ref, k_ref, v_ref, r_ref, gl_ref, wg_ref, bg_ref, gain_ref, out_ref, st_ref,
                *, heads, dk, dv, chunk, nchunks):
    L = chunk

    @pl.when(pl.program_id(1) == 0)
    def _():
        st_ref[...] = jnp.zeros_like(st_ref)

    tri_b = _tril(L, BF16)
    row_id = lax.broadcasted_iota(jnp.int32, (L, 1), 0)
    lane_id = lax.broadcasted_iota(jnp.int32, (L, L), 1)
    qscale = dk ** -0.5

    def chunk_body(c, carry):
        r0 = pl.multiple_of(c * L, L)
        rows = pl.ds(r0, L)
        la = _log_sigmoid(_dot(gl_ref[rows, :].astype(BF16), wg_ref[...]) + bg_ref[...]) / GLA_TAU
        b_all = _cumsum_rows(la, tri_b)
        for h in range(heads):
            b = b_all[:, h * dk:(h + 1) * dk]
            q = q_ref[rows, h * dk:(h + 1) * dk] * qscale
            k = k_ref[rows, h * dk:(h + 1) * dk]
            vb = v_ref[rows, h * dv:(h + 1) * dv].astype(BF16)
            st = st_ref[h]

            att = jnp.zeros((L, L), F32)
            for s in range(L):
                rel = jnp.where(row_id >= s, b - b[s:s + 1, :], NEG_INF)
                col = jnp.sum(q * k[s:s + 1, :] * jnp.exp(rel), axis=-1, keepdims=True)
                att = jnp.where(lane_id == s, col, att)
            b_last = b[L - 1:L, :]
            o = _dot(att.astype(BF16), vb) + _dot((q * jnp.exp(b)).astype(BF16), st.astype(BF16), NT_DIMS)
            kdec = (k * jnp.exp(b_last - b)).astype(BF16)
            st_ref[h] = jnp.exp(b_last) * st + _dot(vb, kdec, TN_DIMS)

            y = o * lax.rsqrt(jnp.mean(o * o, axis=-1, keepdims=True) + EPS) * gain_ref[:, h * dv:(h + 1) * dv]
            r = r_ref[rows, h * dv:(h + 1) * dv]
            y = y * (r * _sigmoid(r))
            out_ref[rows, h * dv:(h + 1) * dv] = y.astype(out_ref.dtype)
        return carry

    lax.fori_loop(0, nchunks, chunk_body, 0)


def gla_block(z, w_gate, b_gate, head_gain, batch, seq, q_off, k_off, v_off, r_off, g_off):
    T = z.shape[0]
    H, L = GLA_HEADS, GLA_CHUNK
    dkt = w_gate.shape[1]
    dvt = head_gain.shape[0]
    dk, dv = dkt // H, dvt // H
    ts = min(seq, 256)
    nblk = seq // ts

    def col(off, w):
        assert off % w == 0
        return pl.BlockSpec((ts, w), lambda b, s: (b * nblk + s, off // w))

    return pl.pallas_call(
        functools.partial(_gla_kernel, heads=H, dk=dk, dv=dv, chunk=L, nchunks=ts // L),
        grid=(batch, nblk),
        in_specs=[col(q_off, dkt), col(k_off, dkt), col(v_off, dvt), col(r_off, dvt), col(g_off, LANE),
                  pl.BlockSpec((LANE, dkt), lambda b, s: (0, 0)),
                  pl.BlockSpec((1, dkt), lambda b, s: (0, 0)),
                  pl.BlockSpec((1, dvt), lambda b, s: (0, 0))],
        out_specs=pl.BlockSpec((ts, dvt), lambda b, s: (b * nblk + s, 0)),
        out_shape=jax.ShapeDtypeStruct((T, dvt), BF16),
        scratch_shapes=[pltpu.VMEM((H, dv, dk), F32)],
        compiler_params=_params("parallel", "arbitrary"),
        name="gla",
    )(z, z, z, z, z, w_gate, b_gate.reshape(1, dkt).astype(F32), head_gain.reshape(1, dvt).astype(F32))


def _extract_topk(vals, k):
    rank = jnp.full(vals.shape, float(k), F32)
    tops = []
    for i in range(k):
        mx = jnp.max(vals, axis=0, keepdims=True)
        hit = vals == mx
        rank = jnp.where(hit, float(i), rank)
        vals = jnp.where(hit, NEG_INF, vals)
        tops.append(mx)
    return jnp.concatenate(tops, axis=0), rank


def _peer_topk_kernel(qt_ref, keys_ref, cnt1_ref, p1_ref, r2_ref, p2_ref, *, heads, half, topk):
    K = topk
    for h in range(heads):
        s1 = _dot(keys_ref[h, 0], qt_ref[(2 * h) * half:(2 * h + 1) * half, :])
        s2 = _dot(keys_ref[h, 1], qt_ref[(2 * h + 1) * half:(2 * h + 2) * half, :])
        sv1, rank1 = _extract_topk(s1, K)
        sv2, rank2 = _extract_topk(s2, K)
        cand = []
        a = 0
        while K // (a + 1) > 1:
            nb = K // (a + 1)
            rows = -(-nb // SUBLANE) * SUBLANE
            c = sv1[a:a + 1, :] + sv2[:rows, :]
            if nb < rows:
                c = jnp.where(lax.broadcasted_iota(jnp.int32, (rows, 1), 0) < nb, c, NEG_INF)
            cand.append(c)
            a += 1
        n_single = a
        cand.append(sv1[n_single:, :] + sv2[0:1, :])
        work = jnp.concatenate(cand, axis=0)
        tau = None
        for _ in range(K):
            tau = jnp.max(work, axis=0, keepdims=True)
            work = jnp.where(work == tau, NEG_INF, work)
        cmax = sv1[0:1, :] + sv2[0:1, :]
        zsum = jnp.zeros_like(cmax)
        cnt1 = jnp.zeros_like(s1)
        for a, c in enumerate(cand):
            sel = c >= tau
            zsum = zsum + jnp.sum(jnp.where(sel, jnp.exp(c - cmax), 0.0), axis=0, keepdims=True)
            if a < n_single:
                cnt1 = jnp.where(rank1 == float(a), jnp.sum(sel.astype(F32), axis=0, keepdims=True), cnt1)
            else:
                for j in range(K - n_single):
                    cnt1 = jnp.where(rank1 == float(a + j), sel[j:j + 1, :].astype(F32), cnt1)
        cnt1_ref[h] = cnt1
        p1_ref[h] = jnp.exp(s1 - sv1[0:1, :]) / zsum
        r2_ref[h] = pltpu.bitcast(rank2.astype(BF16), jnp.uint32)
        p2_ref[h] = pltpu.bitcast(jnp.exp(s2 - sv2[0:1, :]).astype(BF16), jnp.uint32)


def peer_topk(qt, keys):
    heads, _, n_keys, half = keys.shape
    T = qt.shape[1]
    tt = min(T, 256)
    out_f = jax.ShapeDtypeStruct((heads, n_keys, T), F32)
    out_b = jax.ShapeDtypeStruct((heads, n_keys // 2, T), jnp.uint32)
    ospec = pl.BlockSpec((heads, n_keys, tt), lambda i: (0, 0, i))
    ospec_b = pl.BlockSpec((heads, n_keys // 2, tt), lambda i: (0, 0, i))
    return pl.pallas_call(
        functools.partial(_peer_topk_kernel, heads=heads, half=half, topk=PEER_TOPK),
        grid=(T // tt,),
        in_specs=[pl.BlockSpec((heads * 2 * half, tt), lambda i: (0, i)),
                  pl.BlockSpec((heads, 2, n_keys, half), lambda i: (0, 0, 0, 0))],
        out_specs=[ospec, ospec, ospec_b, ospec_b],
        out_shape=[out_f, out_f, out_b, out_b],
        compiler_params=_params("parallel"),
        name="peer_topk",
    )(qt, keys)


def _peer_gate_block(a_ref, coef_ref, blk, cnt1_ref, p1_ref, r2_ref, p2_ref, *, heads, n_keys, rows_per_blk):
    tm = a_ref.shape[1]
    sub = n_keys
    for r in range(rows_per_blk):
        e1 = blk * rows_per_blk + r
        cnt_rows = [cnt1_ref[h, pl.ds(e1, 1), :] for h in range(heads)]
        p1_rows = [p1_ref[h, pl.ds(e1, 1), :] for h in range(heads)]
        for lt in range(tm // LANE):
            lanes = slice(lt * LANE, (lt + 1) * LANE)
            cnts = [jnp.broadcast_to(cnt_rows[h][:, lanes], (sub, LANE)).astype(BF16) for h in range(heads)]
            p1s = [jnp.broadcast_to(p1_rows[h][:, lanes], (sub, LANE)).astype(BF16) for h in range(heads)]
            for sb in range(n_keys // sub):
                rows = slice(r * n_keys + sb * sub, r * n_keys + (sb + 1) * sub)
                words = slice(sb * sub // 2, (sb + 1) * sub // 2)
                a = a_ref[rows, lanes]
                gelu = (0.5 * a * (1.0 + lax.erf(a * (2.0 ** -0.5)))).astype(BF16)
                gate = jnp.zeros((sub, LANE), BF16)
                for h in range(heads):
                    r2 = pltpu.bitcast(r2_ref[h, words, lanes], BF16)
                    p2 = pltpu.bitcast(p2_ref[h, words, lanes], BF16)
                    gate = gate + jnp.where(r2 < cnts[h], p2 * p1s[h], jnp.zeros_like(gate))
                coef_ref[lanes, rows] = (gate * gelu).T


def _peer_dense_kernel(xt_ref, u_ref, v_ref, cnt1_ref, p1_ref, r2_ref, p2_ref, res_ref, out_ref,
                       a0_ref, a1_ref, c0_ref, c1_ref, *, heads, n_keys, rows_per_blk, nblk):
    g = pl.program_id(1)
    eb = rows_per_blk * n_keys
    gate = functools.partial(_peer_gate_block, cnt1_ref=cnt1_ref, p1_ref=p1_ref, r2_ref=r2_ref, p2_ref=p2_ref,
                             heads=heads, n_keys=n_keys, rows_per_blk=rows_per_blk)

    @pl.when(g == 0)
    def _():
        out_ref[...] = res_ref[...]
        a1_ref[...] = jnp.zeros_like(a1_ref)
        c0_ref[...] = jnp.zeros_like(c0_ref)

    a0_ref[...] = _dot(u_ref[:eb, :], xt_ref[...])
    gate(a1_ref, c1_ref, jnp.maximum(2 * g - 1, 0))
    out_ref[...] += _dot(c0_ref[...], v_ref[:eb, :])

    a1_ref[...] = _dot(u_ref[eb:, :], xt_ref[...])
    gate(a0_ref, c0_ref, jnp.minimum(2 * g, nblk - 1))
    out_ref[...] += _dot(c1_ref[...], v_ref[eb:, :])


def peer_dense(xt, u, v, cnt1, p1, r2, p2, residual):
    D, T = xt.shape
    heads, n_keys, _ = cnt1.shape
    E = u.shape[0]
    rows_per_blk = 2
    eb = rows_per_blk * n_keys
    nblk = E // eb
    npair = nblk // 2
    tm = min(T, 512)
    gspec = pl.BlockSpec((heads, n_keys, tm), lambda i, g: (0, 0, i))
    gspec_b = pl.BlockSpec((heads, n_keys // 2, tm), lambda i, g: (0, 0, i))
    return pl.pallas_call(
        functools.partial(_peer_dense_kernel, heads=heads, n_keys=n_keys, rows_per_blk=rows_per_blk, nblk=nblk),
        grid=(T // tm, npair + 1),
        in_specs=[pl.BlockSpec((D, tm), lambda i, g: (0, i)),
                  pl.BlockSpec((2 * eb, D), lambda i, g: (jnp.minimum(g, npair - 1), 0)),
                  pl.BlockSpec((2 * eb, D), lambda i, g: (jnp.maximum(g - 1, 0), 0)),
                  gspec, gspec, gspec_b, gspec_b,
                  pl.BlockSpec((tm, D), lambda i, g: (i, 0), pipeline_mode=pl.Buffered(1))],
        out_specs=pl.BlockSpec((tm, D), lambda i, g: (i, 0)),
        out_shape=jax.ShapeDtypeStruct((T, D), F32),
        scratch_shapes=[pltpu.VMEM((eb, tm), F32), pltpu.VMEM((eb, tm), F32),
                        pltpu.VMEM((tm, eb), BF16), pltpu.VMEM((tm, eb), BF16)],
        compiler_params=_params("parallel", "arbitrary"),
        name="peer_dense",
    )(xt, u, v, cnt1, p1, r2, p2, residual)


def peer_layer(x, norm_g, w_q, sub_keys, u_tab, v_tab):
    ht = rmsnorm(x, norm_g, BF16, transpose=True)
    qt = matmul(w_q.T.astype(BF16), ht, BF16, tn=512, name="peer_q")
    cnt1, p1, r2, p2 = peer_topk(qt, sub_keys.astype(BF16))
    return peer_dense(ht, u_tab.astype(BF16), v_tab.astype(BF16), cnt1, p1, r2, p2, x)


def _pad_cols(w, n):
    return jnp.pad(w, ((0, 0), (0, n - w.shape[1])))


def even_layer(x, batch, seq, norm_g, w_in, b_igate, b_fgate, head_gain, w_pool, pool_scale, w_out):
    H = MLSTM_HEADS
    width = head_gain.shape[0]
    qk = width // 2
    pool_w = pool_scale.shape[0]
    n_main = 2 * qk + 2 * width
    h = rmsnorm(x, norm_g, BF16)
    w_cat = jnp.concatenate([w_in[:, :n_main], w_in[:, n_main + 2 * H:], _pad_cols(w_in[:, n_main:n_main + 2 * H], LANE)],
                            axis=1).astype(BF16)
    z = matmul(h, w_cat, F32, tn=384, name="even_in")
    gate_bias = jnp.pad(jnp.concatenate([b_igate, b_fgate]), (0, LANE - 2 * H)).reshape(1, LANE).astype(F32)
    hm = mlstm_block(z, gate_bias, head_gain, batch, seq, 0, qk, 2 * qk, 2 * qk + width, n_main + pool_w)
    hp = pool_block(z, w_pool, pool_scale, batch, seq, n_main)
    y = jnp.concatenate([hm, hp], axis=1)
    return matmul(y, w_out.astype(BF16), F32, tn=512, residual=x, name="even_out")


def odd_layer(x, batch, seq, norm_g, w_in, w_gate, b_gate, head_gain, w_out):
    rank, dkt = w_gate.shape
    dvt = head_gain.shape[0]
    n_main = 2 * dkt + 2 * dvt
    h = rmsnorm(x, norm_g, BF16)
    w_cat = jnp.concatenate([w_in[:, :n_main], _pad_cols(w_in[:, n_main:], LANE)], axis=1).astype(BF16)
    z = matmul(h, w_cat, F32, tn=896, name="odd_in")
    wg = jnp.pad(w_gate, ((0, LANE - rank), (0, 0))).astype(BF16)
    y = gla_block(z, wg, b_gate, head_gain, batch, seq, 0, dkt, 2 * dkt, 2 * dkt + dvt, n_main)
    return matmul(y, w_out.astype(BF16), F32, tn=512, residual=x, name="odd_out")


def kernel(x, e_norm, e_w_in, e_b_igate, e_b_fgate, e_head_gain, e_w_pool, e_pool_scale, e_w_out, o_norm, o_w_in, o_w_gate, o_b_gate, o_head_gain, o_w_out, f_norm, f_w_q, f_sub_keys, f_u, f_v, final_norm):
    B, S, D = x.shape
    depth = f_norm.shape[0]
    h = x.reshape(B * S, D)
    for layer in range(depth):
        j = layer // 2
        if layer % 2 == 0:
            h = even_layer(h, B, S, e_norm[j], e_w_in[j], e_b_igate[j], e_b_fgate[j], e_head_gain[j],
                           e_w_pool[j], e_pool_scale[j], e_w_out[j])
        else:
            h = odd_layer(h, B, S, o_norm[j], o_w_in[j], o_w_gate[j], o_b_gate[j], o_head_gain[j], o_w_out[j])
        h = peer_layer(h, f_norm[layer], f_w_q[layer], f_sub_keys[layer], f_u[layer], f_v[layer])
    return rmsnorm(h, final_norm, F32).reshape(B, S, D)
```

```python
import functools

import jax
import jax.numpy as jnp
from jax import lax
from jax.experimental import pallas as pl
from jax.experimental.pallas import tpu as pltpu

F32 = jnp.float32
BF16 = jnp.bfloat16
EPS = 1e-6
LANE = 128
SUBLANE = 8
MXU_COLS = 256
VMEM_LIMIT = 56 * 1024 * 1024

MLSTM_HEADS = 4
MLSTM_CHUNK = 64
POOL_WINDOWS = (2, 4, 8, 16)
POOL_HALO = 16
GLA_HEADS = 4
GLA_CHUNK = 32
GLA_TAU = 16.0
PEER_TOPK = 16
NEG_INF = float("-inf")

NT_DIMS = (((1,), (1,)), ((), ()))
TN_DIMS = (((0,), (0,)), ((), ()))


def _params(*sem):
    return pltpu.CompilerParams(dimension_semantics=sem, vmem_limit_bytes=VMEM_LIMIT)


def _log_sigmoid(x):
    return jnp.minimum(x, 0.0) - jnp.log1p(jnp.exp(-jnp.abs(x)))


def _sigmoid(x):
    return 1.0 / (1.0 + jnp.exp(-x))


def _dot(a, b, dims=None):
    if dims is None:
        return jnp.dot(a, b, preferred_element_type=F32)
    return lax.dot_general(a, b, dims, preferred_element_type=F32)


def _cumsum_rows(x, tri):
    hi = x.astype(BF16)
    lo = (x - hi.astype(F32)).astype(BF16)
    return _dot(tri, hi) + _dot(tri, lo)


def _tril(n, dtype):
    r = lax.broadcasted_iota(jnp.int32, (n, n), 0)
    c = lax.broadcasted_iota(jnp.int32, (n, n), 1)
    return (c <= r).astype(dtype)


def _rms_kernel(x_ref, g_ref, o_ref, *, transpose):
    x = x_ref[...]
    y = x * lax.rsqrt(jnp.mean(x * x, axis=-1, keepdims=True) + EPS) * g_ref[...]
    if transpose:
        y = y.T
    o_ref[...] = y.astype(o_ref.dtype)


def rmsnorm(x, g, out_dtype, transpose=False):
    T, D = x.shape
    tm = min(T, 512)
    if transpose:
        out_shape, out_spec = (D, T), pl.BlockSpec((D, tm), lambda i: (0, i))
    else:
        out_shape, out_spec = (T, D), pl.BlockSpec((tm, D), lambda i: (i, 0))
    return pl.pallas_call(
        functools.partial(_rms_kernel, transpose=transpose),
        grid=(T // tm,),
        in_specs=[pl.BlockSpec((tm, D), lambda i: (i, 0)), pl.BlockSpec((1, D), lambda i: (0, 0))],
        out_specs=out_spec,
        out_shape=jax.ShapeDtypeStruct(out_shape, out_dtype),
        compiler_params=_params("parallel"),
        name="rmsnorm_t" if transpose else "rmsnorm",
    )(x, g.reshape(1, D).astype(F32))


def _mm_kernel(a_ref, b_ref, o_ref):
    o_ref[...] = _dot(a_ref[...], b_ref[...]).astype(o_ref.dtype)


def _mm_res_kernel(a_ref, b_ref, r_ref, o_ref):
    o_ref[...] = (_dot(a_ref[...], b_ref[...]) + r_ref[...]).astype(o_ref.dtype)


def matmul(a, b, out_dtype, tn, residual=None, name="matmul"):
    M, K = a.shape
    N = b.shape[1]
    tm = min(M, 1024)
    tn = min(N, tn)
    in_specs = [pl.BlockSpec((tm, K), lambda i, j: (i, 0)), pl.BlockSpec((K, tn), lambda i, j: (0, j))]
    args = [a, b]
    body = _mm_kernel
    if residual is not None:
        in_specs.append(pl.BlockSpec((tm, tn), lambda i, j: (i, j)))
        args.append(residual)
        body = _mm_res_kernel
    return pl.pallas_call(
        body,
        grid=(M // tm, N // tn),
        in_specs=in_specs,
        out_specs=pl.BlockSpec((tm, tn), lambda i, j: (i, j)),
        out_shape=jax.ShapeDtypeStruct((M, N), out_dtype),
        compiler_params=_params("parallel", "parallel"),
        name=name,
    )(*args)


def _mlstm_kernel(q_ref, k_ref, v_ref, o_ref, g_ref, gb_ref, gain_ref, out_ref, cn_ref, m_ref,
                  *, batch, heads, dk, dv, chunk, nchunks):
    L = chunk

    @pl.when(pl.program_id(0) == 0)
    def _():
        cn_ref[...] = jnp.zeros_like(cn_ref)
        m_ref[...] = jnp.zeros_like(m_ref)

    tri_b = _tril(L, BF16)
    causal = _tril(L, jnp.int32) > 0
    ones_col = (lax.broadcasted_iota(jnp.int32, (L, LANE), 1) == 0).astype(F32)
    kscale = dk ** -0.5

    def chunk_body(c, carry):
        r0 = pl.multiple_of(c * L, L)
        rows = pl.ds(r0, L)
        for b in range(batch):
            g = g_ref[b, rows, :] + gb_ref[...]
            bc = _cumsum_rows(_log_sigmoid(g), tri_b)
            g_t = g.T
            bc_t = bc.T
            for h in range(heads):
                st = b * heads + h
                i_col = g[:, h:h + 1]
                i_row = g_t[h:h + 1, :]
                b_col = bc[:, heads + h:heads + h + 1]
                b_row = bc_t[heads + h:heads + h + 1, :]
                m_prev = m_ref[st:st + 1, 0:1]
                qb = q_ref[b, rows, h * dk:(h + 1) * dk].astype(BF16)
                kb = (k_ref[b, rows, h * dk:(h + 1) * dk] * kscale).astype(BF16)
                v_aug = jnp.concatenate([v_ref[b, rows, h * dv:(h + 1) * dv], ones_col], axis=1)
                cn = cn_ref[st]

                dmat = jnp.where(causal, b_col - b_row + i_row, NEG_INF)
                inter = b_col + m_prev
                m_t = jnp.maximum(jnp.max(dmat, axis=-1, keepdims=True), inter)
                a_inter = jnp.exp(inter - m_t)
                s = _dot(qb, kb, NT_DIMS) * jnp.exp(dmat - m_t)
                hn = _dot(s.astype(BF16), v_aug.astype(BF16)) + a_inter * _dot(qb, cn.astype(BF16), NT_DIMS)
                num = hn[:, :dv]
                den = hn[:, dv:dv + 1]
                hh = num / jnp.maximum(jnp.abs(den), jnp.exp(-m_t))

                b_last = b_col[L - 1:L, :]
                dec = b_last - b_col + i_col
                m_new = jnp.maximum(b_last + m_prev, jnp.max(dec, axis=0, keepdims=True))
                wk = jnp.exp(dec - m_new)
                sc = jnp.exp(b_last + m_prev - m_new)
                cn_ref[st] = sc * cn + _dot((wk * v_aug).astype(BF16), kb, TN_DIMS)
                m_ref[st:st + 1, :] = jnp.broadcast_to(m_new, (1, LANE))

                y = hh * lax.rsqrt(jnp.mean(hh * hh, axis=-1, keepdims=True) + EPS) * gain_ref[:, h * dv:(h + 1) * dv]
                y = y * _sigmoid(o_ref[b, rows, h * dv:(h + 1) * dv])
                out_ref[b, rows, h * dv:(h + 1) * dv] = y.astype(out_ref.dtype)
        return carry

    lax.fori_loop(0, nchunks, chunk_body, 0)


def mlstm_block(z, gate_bias, head_gain, batch, seq, q_off, k_off, v_off, o_off, g_off):
    T, cols = z.shape
    H, L = MLSTM_HEADS, MLSTM_CHUNK
    width = head_gain.shape[0]
    dv = width // H
    dk = dv // 2
    ts = min(seq, 256)
    z3 = z.reshape(batch, seq, cols)

    def col(off, w):
        assert off % w == 0
        return pl.BlockSpec((batch, ts, w), lambda s: (0, s, off // w))

    n_state = batch * H
    out = pl.pallas_call(
        functools.partial(_mlstm_kernel, batch=batch, heads=H, dk=dk, dv=dv, chunk=L, nchunks=ts // L),
        grid=(seq // ts,),
        in_specs=[col(q_off, H * dk), col(k_off, H * dk), col(v_off, width), col(o_off, width), col(g_off, LANE),
                  pl.BlockSpec((1, LANE), lambda s: (0, 0)), pl.BlockSpec((1, width), lambda s: (0, 0))],
        out_specs=pl.BlockSpec((batch, ts, width), lambda s: (0, s, 0)),
        out_shape=jax.ShapeDtypeStruct((batch, seq, width), BF16),
        scratch_shapes=[pltpu.VMEM((n_state, dv + LANE, dk), F32),
                        pltpu.VMEM((-(-n_state // SUBLANE) * SUBLANE, LANE), F32)],
        compiler_params=_params("arbitrary"),
        name="mlstm",
    )(z3, z3, z3, z3, z3, gate_bias, head_gain.reshape(1, width).astype(F32))
    return out.reshape(T, width)


def _pool_kernel(p_ref, w_ref, s_ref, out_ref, halo_ref, *, group, ts):
    si = pl.program_id(1)

    @pl.when(si == 0)
    def _():
        halo_ref[...] = jnp.zeros_like(halo_ref)

    p = p_ref[...]
    buf = jnp.concatenate([halo_ref[...], p], axis=0)
    t_abs = si * ts + lax.broadcasted_iota(jnp.int32, (ts, 1), 0)
    win = buf
    span = 1
    for g, w in enumerate(POOL_WINDOWS):
        while span < w:
            win = win + pltpu.roll(win, span, 0)
            span *= 2
        sl = slice(g * group, (g + 1) * group)
        cnt = jnp.minimum(t_abs + 1, w).astype(F32)
        mixed = win[POOL_HALO:, sl] / cnt - p[:, sl]
        y = _dot(mixed.astype(BF16), w_ref[g]) * s_ref[:, sl]
        out_ref[:, sl] = y.astype(out_ref.dtype)
    halo_ref[...] = p[ts - POOL_HALO:, :]


def pool_block(z, w_pool, pool_scale, batch, seq, p_off):
    T = z.shape[0]
    ng, group, _ = w_pool.shape
    width = ng * group
    ts = min(seq, 512)
    nblk = seq // ts
    assert p_off % width == 0
    return pl.pallas_call(
        functools.partial(_pool_kernel, group=group, ts=ts),
        grid=(batch, nblk),
        in_specs=[pl.BlockSpec((ts, width), lambda b, s: (b * nblk + s, p_off // width)),
                  pl.BlockSpec((ng, group, group), lambda b, s: (0, 0, 0)),
                  pl.BlockSpec((1, width), lambda b, s: (0, 0))],
        out_specs=pl.BlockSpec((ts, width), lambda b, s: (b * nblk + s, 0)),
        out_shape=jax.ShapeDtypeStruct((T, width), BF16),
        scratch_shapes=[pltpu.VMEM((POOL_HALO, width), F32)],
        compiler_params=_params("parallel", "arbitrary"),
        name="pool",
    )(z, w_pool.astype(BF16), pool_scale.reshape(1, width).astype(F32))


def _gla_kernel(q_ref, k_ref, v_ref, r_ref, gl_ref, wg_ref, bg_ref, gain_ref, out_ref, st_ref,
                *, heads, dk, dv, chunk, nchunks):
    L = chunk

    @pl.when(pl.program_id(1) == 0)
    def _():
        st_ref[...] = jnp.zeros_like(st_ref)

    tri_b = _tril(L, BF16)
    row_id = lax.broadcasted_iota(jnp.int32, (L, 1), 0)
    lane_id = lax.broadcasted_iota(jnp.int32, (L, L), 1)
    qscale = dk ** -0.5

    def chunk_body(c, carry):
        r0 = pl.multiple_of(c * L, L)
        rows = pl.ds(r0, L)
        la = _log_sigmoid(_dot(gl_ref[rows, :].astype(BF16), wg_ref[...]) + bg_ref[...]) / GLA_TAU
        b_all = _cumsum_rows(la, tri_b)
        for h in range(heads):
            b = b_all[:, h * dk:(h + 1) * dk]
            q = q_ref[rows, h * dk:(h + 1) * dk] * qscale
            k = k_ref[rows, h * dk:(h + 1) * dk]
            vb = v_ref[rows, h * dv:(h + 1) * dv].astype(BF16)
            st = st_ref[h]

            att = jnp.zeros((L, L), F32)
            for s in range(L):
                rel = jnp.where(row_id >= s, b - b[s:s + 1, :], NEG_INF)
                col = jnp.sum(q * k[s:s + 1, :] * jnp.exp(rel), axis=-1, keepdims=True)
                att = jnp.where(lane_id == s, col, att)
            b_last = b[L - 1:L, :]
            o = _dot(att.astype(BF16), vb) + _dot((q * jnp.exp(b)).astype(BF16), st.astype(BF16), NT_DIMS)
            kdec = (k * jnp.exp(b_last - b)).astype(BF16)
            st_ref[h] = jnp.exp(b_last) * st + _dot(vb, kdec, TN_DIMS)

            y = o * lax.rsqrt(jnp.mean(o * o, axis=-1, keepdims=True) + EPS) * gain_ref[:, h * dv:(h + 1) * dv]
            r = r_ref[rows, h * dv:(h + 1) * dv]
            y = y * (r * _sigmoid(r))
            out_ref[rows, h * dv:(h + 1) * dv] = y.astype(out_ref.dtype)
        return carry

    lax.fori_loop(0, nchunks, chunk_body, 0)


def gla_block(z, w_gate, b_gate, head_gain, batch, seq, q_off, k_off, v_off, r_off, g_off):
    T = z.shape[0]
    H, L = GLA_HEADS, GLA_CHUNK
    dkt = w_gate.shape[1]
    dvt = head_gain.shape[0]
    dk, dv = dkt // H, dvt // H
    ts = min(seq, 256)
    nblk = seq // ts

    def col(off, w):
        assert off % w == 0
        return pl.BlockSpec((ts, w), lambda b, s: (b * nblk + s, off // w))

    return pl.pallas_call(
        functools.partial(_gla_kernel, heads=H, dk=dk, dv=dv, chunk=L, nchunks=ts // L),
        grid=(batch, nblk),
        in_specs=[col(q_off, dkt), col(k_off, dkt), col(v_off, dvt), col(r_off, dvt), col(g_off, LANE),
                  pl.BlockSpec((LANE, dkt), lambda b, s: (0, 0)),
                  pl.BlockSpec((1, dkt), lambda b, s: (0, 0)),
                  pl.BlockSpec((1, dvt), lambda b, s: (0, 0))],
        out_specs=pl.BlockSpec((ts, dvt), lambda b, s: (b * nblk + s, 0)),
        out_shape=jax.ShapeDtypeStruct((T, dvt), BF16),
        scratch_shapes=[pltpu.VMEM((H, dv, dk), F32)],
        compiler_params=_params("parallel", "arbitrary"),
        name="gla",
    )(z, z, z, z, z, w_gate, b_gate.reshape(1, dkt).astype(F32), head_gain.reshape(1, dvt).astype(F32))


def _extract_topk(vals, k):
    rank = jnp.full(vals.shape, float(k), F32)
    tops = []
    for i in range(k):
        mx = jnp.max(vals, axis=0, keepdims=True)
        hit = vals == mx
        rank = jnp.where(hit, float(i), rank)
        vals = jnp.where(hit, NEG_INF, vals)
        tops.append(mx)
    return jnp.concatenate(tops, axis=0), rank


def _peer_topk_kernel(qt_ref, keys_ref, cnt1_ref, p1_ref, r2_ref, p2_ref, *, heads, half, topk):
    K = topk
    for h in range(heads):
        s1 = _dot(keys_ref[h, 0], qt_ref[(2 * h) * half:(2 * h + 1) * half, :])
        s2 = _dot(keys_ref[h, 1], qt_ref[(2 * h + 1) * half:(2 * h + 2) * half, :])
        sv1, rank1 = _extract_topk(s1, K)
        sv2, rank2 = _extract_topk(s2, K)
        cand = []
        a = 0
        while K // (a + 1) > 1:
            nb = K // (a + 1)
            rows = -(-nb // SUBLANE) * SUBLANE
            c = sv1[a:a + 1, :] + sv2[:rows, :]
            if nb < rows:
                c = jnp.where(lax.broadcasted_iota(jnp.int32, (rows, 1), 0) < nb, c, NEG_INF)
            cand.append(c)
            a += 1
        n_single = a
        cand.append(sv1[n_single:, :] + sv2[0:1, :])
        work = jnp.concatenate(cand, axis=0)
        tau = None
        for _ in range(K):
            tau = jnp.max(work, axis=0, keepdims=True)
            work = jnp.where(work == tau, NEG_INF, work)
        cmax = sv1[0:1, :] + sv2[0:1, :]
        zsum = jnp.zeros_like(cmax)
        cnt1 = jnp.zeros_like(s1)
        for a, c in enumerate(cand):
            sel = c >= tau
            zsum = zsum + jnp.sum(jnp.where(sel, jnp.exp(c - cmax), 0.0), axis=0, keepdims=True)
            if a < n_single:
                cnt1 = jnp.where(rank1 == float(a), jnp.sum(sel.astype(F32), axis=0, keepdims=True), cnt1)
            else:
                for j in range(K - n_single):
                    cnt1 = jnp.where(rank1 == float(a + j), sel[j:j + 1, :].astype(F32), cnt1)
        cnt1_ref[h] = cnt1
        p1_ref[h] = jnp.exp(s1 - sv1[0:1, :]) / zsum
        r2_ref[h] = pltpu.bitcast(rank2.astype(BF16), jnp.uint32)
        p2_ref[h] = pltpu.bitcast(jnp.exp(s2 - sv2[0:1, :]).astype(BF16), jnp.uint32)


def peer_topk(qt, keys):
    heads, _, n_keys, half = keys.shape
    T = qt.shape[1]
    tt = min(T, 256)
    out_f = jax.ShapeDtypeStruct((heads, n_keys, T), F32)
    out_b = jax.ShapeDtypeStruct((heads, n_keys // 2, T), jnp.uint32)
    ospec = pl.BlockSpec((heads, n_keys, tt), lambda i: (0, 0, i))
    ospec_b = pl.BlockSpec((heads, n_keys // 2, tt), lambda i: (0, 0, i))
    return pl.pallas_call(
        functools.partial(_peer_topk_kernel, heads=heads, half=half, topk=PEER_TOPK),
        grid=(T // tt,),
        in_specs=[pl.BlockSpec((heads * 2 * half, tt), lambda i: (0, i)),
                  pl.BlockSpec((heads, 2, n_keys, half), lambda i: (0, 0, 0, 0))],
        out_specs=[ospec, ospec, ospec_b, ospec_b],
        out_shape=[out_f, out_f, out_b, out_b],
        compiler_params=_params("parallel"),
        name="peer_topk",
    )(qt, keys)


def _peer_gate_tile(a, cnt_rows, p1_rows, r2_ref, p2_ref, lanes, *, heads, n_keys):
    gelu = (0.5 * a * (1.0 + lax.erf(a * (2.0 ** -0.5)))).astype(BF16)
    gate = jnp.zeros((n_keys, LANE), BF16)
    for h in range(heads):
        cnt = jnp.broadcast_to(cnt_rows[h][:, lanes], (n_keys, LANE)).astype(BF16)
        p1 = jnp.broadcast_to(p1_rows[h][:, lanes], (n_keys, LANE)).astype(BF16)
        r2 = pltpu.bitcast(r2_ref[h, :, lanes], BF16)
        p2 = pltpu.bitcast(p2_ref[h, :, lanes], BF16)
        gate = gate + jnp.where(r2 < cnt, p2 * p1, jnp.zeros_like(gate))
    return (gate * gelu).T


def _peer_dense_step(blk, c_new, c_old, xt_ref, u_ref, v_ref, cnt1_ref, p1_ref, r2_ref, p2_ref, out_ref,
                     *, heads, n_keys, groups):
    tm = xt_ref.shape[1]
    dc = out_ref.shape[1] // groups
    for r in range(groups):
        e1 = blk * groups + r
        rows = slice(r * n_keys, (r + 1) * n_keys)
        a = _dot(u_ref[rows, :], xt_ref[...])
        cnt_rows = [cnt1_ref[h, pl.ds(e1, 1), :] for h in range(heads)]
        p1_rows = [p1_ref[h, pl.ds(e1, 1), :] for h in range(heads)]
        for lt in range(tm // LANE):
            lanes = slice(lt * LANE, (lt + 1) * LANE)
            c_new[lanes, rows] = _peer_gate_tile(a[:, lanes], cnt_rows, p1_rows, r2_ref, p2_ref, lanes,
                                                 heads=heads, n_keys=n_keys)
        cols = slice(r * dc, (r + 1) * dc)
        out_ref[:, cols] += _dot(c_old[...], v_ref[:, cols])


def _peer_dense_kernel(xt_ref, u_ref, v_ref, cnt1_ref, p1_ref, r2_ref, p2_ref, res_ref, out_ref, c0_ref, c1_ref,
                       *, heads, n_keys, groups, nblk):
    g = pl.program_id(1)
    step = functools.partial(_peer_dense_step, jnp.minimum(g, nblk - 1), xt_ref=xt_ref, u_ref=u_ref, v_ref=v_ref,
                             cnt1_ref=cnt1_ref, p1_ref=p1_ref, r2_ref=r2_ref, p2_ref=p2_ref, out_ref=out_ref,
                             heads=heads, n_keys=n_keys, groups=groups)

    @pl.when(g == 0)
    def _():
        out_ref[...] = res_ref[...]
        c1_ref[...] = jnp.zeros_like(c1_ref)

    @pl.when(lax.rem(g, 2) == 0)
    def _():
        step(c0_ref, c1_ref)

    @pl.when(lax.rem(g, 2) == 1)
    def _():
        step(c1_ref, c0_ref)


def peer_dense(xt, u, v, cnt1, p1, r2, p2, residual):
    D, T = xt.shape
    heads, n_keys, _ = cnt1.shape
    E = u.shape[0]
    groups = 4
    eb = groups * n_keys
    nblk = E // eb
    tm = min(T, 512)
    gspec = pl.BlockSpec((heads, n_keys, tm), lambda i, g: (0, 0, i))
    gspec_b = pl.BlockSpec((heads, n_keys // 2, tm), lambda i, g: (0, 0, i))
    return pl.pallas_call(
        functools.partial(_peer_dense_kernel, heads=heads, n_keys=n_keys, groups=groups, nblk=nblk),
        grid=(T // tm, nblk + 1),
        in_specs=[pl.BlockSpec((D, tm), lambda i, g: (0, i)),
                  pl.BlockSpec((eb, D), lambda i, g: (jnp.minimum(g, nblk - 1), 0)),
                  pl.BlockSpec((eb, D), lambda i, g: (jnp.maximum(g - 1, 0), 0)),
                  gspec, gspec, gspec_b, gspec_b,
                  pl.BlockSpec((tm, D), lambda i, g: (i, 0), pipeline_mode=pl.Buffered(1))],
        out_specs=pl.BlockSpec((tm, D), lambda i, g: (i, 0)),
        out_shape=jax.ShapeDtypeStruct((T, D), F32),
        scratch_shapes=[pltpu.VMEM((tm, eb), BF16), pltpu.VMEM((tm, eb), BF16)],
        compiler_params=_params("parallel", "arbitrary"),
        name="peer_dense",
    )(xt, u, v, cnt1, p1, r2, p2, residual)


def peer_layer(x, norm_g, w_q, sub_keys, u_tab, v_tab):
    ht = rmsnorm(x, norm_g, BF16, transpose=True)
    qt = matmul(w_q.T.astype(BF16), ht, BF16, tn=512, name="peer_q")
    cnt1, p1, r2, p2 = peer_topk(qt, sub_keys.astype(BF16))
    return peer_dense(ht, u_tab.astype(BF16), v_tab.astype(BF16), cnt1, p1, r2, p2, x)


def _pad_cols(w, n):
    return jnp.pad(w, ((0, 0), (0, n - w.shape[1])))


def even_layer(x, batch, seq, norm_g, w_in, b_igate, b_fgate, head_gain, w_pool, pool_scale, w_out):
    H = MLSTM_HEADS
    width = head_gain.shape[0]
    qk = width // 2
    pool_w = pool_scale.shape[0]
    n_main = 2 * qk + 2 * width
    h = rmsnorm(x, norm_g, BF16)
    w_cat = jnp.concatenate([w_in[:, :n_main], w_in[:, n_main + 2 * H:], w_in[:, n_main:n_main + 2 * H]], axis=1)
    tn = 3 * MXU_COLS
    z = matmul(h, _pad_cols(w_cat, -(-w_cat.shape[1] // tn) * tn).astype(BF16), F32, tn=tn, name="even_in")
    gate_bias = jnp.pad(jnp.concatenate([b_igate, b_fgate]), (0, LANE - 2 * H)).reshape(1, LANE).astype(F32)
    hm = mlstm_block(z, gate_bias, head_gain, batch, seq, 0, qk, 2 * qk, 2 * qk + width, n_main + pool_w)
    hp = pool_block(z, w_pool, pool_scale, batch, seq, n_main)
    y = jnp.concatenate([hm, hp], axis=1)
    return matmul(y, w_out.astype(BF16), F32, tn=4 * MXU_COLS, residual=x, name="even_out")


def odd_layer(x, batch, seq, norm_g, w_in, w_gate, b_gate, head_gain, w_out):
    rank, dkt = w_gate.shape
    dvt = head_gain.shape[0]
    n_main = 2 * dkt + 2 * dvt
    h = rmsnorm(x, norm_g, BF16)
    tn = 5 * MXU_COLS
    z = matmul(h, _pad_cols(w_in, -(-w_in.shape[1] // tn) * tn).astype(BF16), F32, tn=tn, name="odd_in")
    wg = jnp.pad(w_gate, ((0, LANE - rank), (0, 0))).astype(BF16)
    y = gla_block(z, wg, b_gate, head_gain, batch, seq, 0, dkt, 2 * dkt, 2 * dkt + dvt, n_main)
    return matmul(y, w_out.astype(BF16), F32, tn=4 * MXU_COLS, residual=x, name="odd_out")


def kernel(x, e_norm, e_w_in, e_b_igate, e_b_fgate, e_head_gain, e_w_pool, e_pool_scale, e_w_out, o_norm, o_w_in, o_w_gate, o_b_gate, o_head_gain, o_w_out, f_norm, f_w_q, f_sub_keys, f_u, f_v, final_norm):
    B, S, D = x.shape
    depth = f_norm.shape[0]
    h = x.reshape(B * S, D)
    for layer in range(depth):
        j = layer // 2
        if layer % 2 == 0:
            h = even_layer(h, B, S, e_norm[j], e_w_in[j], e_b_igate[j], e_b_fgate[j], e_head_gain[j],
                           e_w_pool[j], e_pool_scale[j], e_w_out[j])
        else:
            h = odd_layer(h, B, S, o_norm[j], o_w_in[j], o_w_gate[j], o_b_gate[j], o_head_gain[j], o_w_out[j])
        h = peer_layer(h, f_norm[layer], f_w_q[layer], f_sub_keys[layer], f_u[layer], f_v[layer])
    return rmsnorm(h, final_norm, F32).reshape(B, S, D)
```

```python
import functools

import jax
import jax.numpy as jnp
from jax import lax
from jax.experimental import pallas as pl
from jax.experimental.pallas import tpu as pltpu

F32 = jnp.float32
BF16 = jnp.bfloat16
EPS = 1e-6
LANE = 128
SUBLANE = 8
MXU_COLS = 256
VMEM_LIMIT = 56 * 1024 * 1024

MLSTM_HEADS = 4
MLSTM_CHUNK = 64
POOL_WINDOWS = (2, 4, 8, 16)
POOL_HALO = 16
GLA_HEADS = 4
GLA_CHUNK = 32
GLA_TAU = 16.0
PEER_TOPK = 16
NEG_INF = float("-inf")

NT_DIMS = (((1,), (1,)), ((), ()))
TN_DIMS = (((0,), (0,)), ((), ()))


def _params(*sem):
    return pltpu.CompilerParams(dimension_semantics=sem, vmem_limit_bytes=VMEM_LIMIT)


def _log_sigmoid(x):
    return jnp.minimum(x, 0.0) - jnp.log1p(jnp.exp(-jnp.abs(x)))


def _sigmoid(x):
    return 1.0 / (1.0 + jnp.exp(-x))


def _dot(a, b, dims=None):
    if dims is None:
        return jnp.dot(a, b, preferred_element_type=F32)
    return lax.dot_general(a, b, dims, preferred_element_type=F32)


def _cumsum_rows(x, tri):
    hi = x.astype(BF16)
    lo = (x - hi.astype(F32)).astype(BF16)
    return _dot(tri, hi) + _dot(tri, lo)


def _tril(n, dtype):
    r = lax.broadcasted_iota(jnp.int32, (n, n), 0)
    c = lax.broadcasted_iota(jnp.int32, (n, n), 1)
    return (c <= r).astype(dtype)


def _rms_kernel(x_ref, g_ref, o_ref, *, transpose):
    x = x_ref[...]
    y = x * lax.rsqrt(jnp.mean(x * x, axis=-1, keepdims=True) + EPS) * g_ref[...]
    if transpose:
        y = y.T
    o_ref[...] = y.astype(o_ref.dtype)


def rmsnorm(x, g, out_dtype, transpose=False):
    T, D = x.shape
    tm = min(T, 512)
    if transpose:
        out_shape, out_spec = (D, T), pl.BlockSpec((D, tm), lambda i: (0, i))
    else:
        out_shape, out_spec = (T, D), pl.BlockSpec((tm, D), lambda i: (i, 0))
    return pl.pallas_call(
        functools.partial(_rms_kernel, transpose=transpose),
        grid=(T // tm,),
        in_specs=[pl.BlockSpec((tm, D), lambda i: (i, 0)), pl.BlockSpec((1, D), lambda i: (0, 0))],
        out_specs=out_spec,
        out_shape=jax.ShapeDtypeStruct(out_shape, out_dtype),
        compiler_params=_params("parallel"),
        name="rmsnorm_t" if transpose else "rmsnorm",
    )(x, g.reshape(1, D).astype(F32))


def _mm_kernel(a_ref, b_ref, o_ref):
    o_ref[...] = _dot(a_ref[...], b_ref[...]).astype(o_ref.dtype)


def _mm_res_kernel(a_ref, b_ref, r_ref, o_ref):
    o_ref[...] = (_dot(a_ref[...], b_ref[...]) + r_ref[...]).astype(o_ref.dtype)


def matmul(a, b, out_dtype, tn, residual=None, name="matmul"):
    M, K = a.shape
    N = b.shape[1]
    tm = min(M, 1024)
    tn = min(N, tn)
    in_specs = [pl.BlockSpec((tm, K), lambda i, j: (i, 0)), pl.BlockSpec((K, tn), lambda i, j: (0, j))]
    args = [a, b]
    body = _mm_kernel
    if residual is not None:
        in_specs.append(pl.BlockSpec((tm, tn), lambda i, j: (i, j)))
        args.append(residual)
        body = _mm_res_kernel
    return pl.pallas_call(
        body,
        grid=(M // tm, N // tn),
        in_specs=in_specs,
        out_specs=pl.BlockSpec((tm, tn), lambda i, j: (i, j)),
        out_shape=jax.ShapeDtypeStruct((M, N), out_dtype),
        compiler_params=_params("parallel", "parallel"),
        name=name,
    )(*args)


def _mlstm_kernel(q_ref, k_ref, v_ref, o_ref, g_ref, gb_ref, gain_ref, out_ref, cn_ref, m_ref,
                  *, batch, heads, dk, dv, chunk, nchunks):
    L = chunk

    @pl.when(pl.program_id(0) == 0)
    def _():
        cn_ref[...] = jnp.zeros_like(cn_ref)
        m_ref[...] = jnp.zeros_like(m_ref)

    tri_b = _tril(L, BF16)
    causal = _tril(L, jnp.int32) > 0
    ones_col = (lax.broadcasted_iota(jnp.int32, (L, LANE), 1) == 0).astype(F32)
    kscale = dk ** -0.5

    def chunk_body(c, carry):
        r0 = pl.multiple_of(c * L, L)
        rows = pl.ds(r0, L)
        for b in range(batch):
            g = g_ref[b, rows, :] + gb_ref[...]
            bc = _cumsum_rows(_log_sigmoid(g), tri_b)
            g_t = g.T
            bc_t = bc.T
            for h in range(heads):
                st = b * heads + h
                i_col = g[:, h:h + 1]
                i_row = g_t[h:h + 1, :]
                b_col = bc[:, heads + h:heads + h + 1]
                b_row = bc_t[heads + h:heads + h + 1, :]
                m_prev = m_ref[st:st + 1, 0:1]
                qb = q_ref[b, rows, h * dk:(h + 1) * dk].astype(BF16)
                kb = (k_ref[b, rows, h * dk:(h + 1) * dk] * kscale).astype(BF16)
                v_aug = jnp.concatenate([v_ref[b, rows, h * dv:(h + 1) * dv], ones_col], axis=1)
                cn = cn_ref[st]

                dmat = jnp.where(causal, b_col - b_row + i_row, NEG_INF)
                inter = b_col + m_prev
                m_t = jnp.maximum(jnp.max(dmat, axis=-1, keepdims=True), inter)
                a_inter = jnp.exp(inter - m_t)
                s = _dot(qb, kb, NT_DIMS) * jnp.exp(dmat - m_t)
                hn = _dot(s.astype(BF16), v_aug.astype(BF16)) + a_inter * _dot(qb, cn.astype(BF16), NT_DIMS)
                num = hn[:, :dv]
                den = hn[:, dv:dv + 1]
                hh = num / jnp.maximum(jnp.abs(den), jnp.exp(-m_t))

                b_last = b_col[L - 1:L, :]
                dec = b_last - b_col + i_col
                m_new = jnp.maximum(b_last + m_prev, jnp.max(dec, axis=0, keepdims=True))
                wk = jnp.exp(dec - m_new)
                sc = jnp.exp(b_last + m_prev - m_new)
                cn_ref[st] = sc * cn + _dot((wk * v_aug).astype(BF16), kb, TN_DIMS)
                m_ref[st:st + 1, :] = jnp.broadcast_to(m_new, (1, LANE))

                y = hh * lax.rsqrt(jnp.mean(hh * hh, axis=-1, keepdims=True) + EPS) * gain_ref[:, h * dv:(h + 1) * dv]
                y = y * _sigmoid(o_ref[b, rows, h * dv:(h + 1) * dv])
                out_ref[b, rows, h * dv:(h + 1) * dv] = y.astype(out_ref.dtype)
        return carry

    lax.fori_loop(0, nchunks, chunk_body, 0)


def mlstm_block(z, gate_bias, head_gain, batch, seq, q_off, k_off, v_off, o_off, g_off):
    T, cols = z.shape
    H, L = MLSTM_HEADS, MLSTM_CHUNK
    width = head_gain.shape[0]
    dv = width // H
    dk = dv // 2
    ts = min(seq, 256)
    z3 = z.reshape(batch, seq, cols)

    def col(off, w):
        assert off % w == 0
        return pl.BlockSpec((batch, ts, w), lambda s: (0, s, off // w))

    n_state = batch * H
    out = pl.pallas_call(
        functools.partial(_mlstm_kernel, batch=batch, heads=H, dk=dk, dv=dv, chunk=L, nchunks=ts // L),
        grid=(seq // ts,),
        in_specs=[col(q_off, H * dk), col(k_off, H * dk), col(v_off, width), col(o_off, width), col(g_off, LANE),
                  pl.BlockSpec((1, LANE), lambda s: (0, 0)), pl.BlockSpec((1, width), lambda s: (0, 0))],
        out_specs=pl.BlockSpec((batch, ts, width), lambda s: (0, s, 0)),
        out_shape=jax.ShapeDtypeStruct((batch, seq, width), BF16),
        scratch_shapes=[pltpu.VMEM((n_state, dv + LANE, dk), F32),
                        pltpu.VMEM((-(-n_state // SUBLANE) * SUBLANE, LANE), F32)],
        compiler_params=_params("arbitrary"),
        name="mlstm",
    )(z3, z3, z3, z3, z3, gate_bias, head_gain.reshape(1, width).astype(F32))
    return out.reshape(T, width)


def _pool_kernel(p_ref, w_ref, s_ref, out_ref, halo_ref, *, group, ts):
    si = pl.program_id(1)

    @pl.when(si == 0)
    def _():
        halo_ref[...] = jnp.zeros_like(halo_ref)

    p = p_ref[...]
    buf = jnp.concatenate([halo_ref[...], p], axis=0)
    t_abs = si * ts + lax.broadcasted_iota(jnp.int32, (ts, 1), 0)
    win = buf
    span = 1
    for g, w in enumerate(POOL_WINDOWS):
        while span < w:
            win = win + pltpu.roll(win, span, 0)
            span *= 2
        sl = slice(g * group, (g + 1) * group)
        cnt = jnp.minimum(t_abs + 1, w).astype(F32)
        mixed = win[POOL_HALO:, sl] / cnt - p[:, sl]
        y = _dot(mixed.astype(BF16), w_ref[g]) * s_ref[:, sl]
        out_ref[:, sl] = y.astype(out_ref.dtype)
    halo_ref[...] = p[ts - POOL_HALO:, :]


def pool_block(z, w_pool, pool_scale, batch, seq, p_off):
    T = z.shape[0]
    ng, group, _ = w_pool.shape
    width = ng * group
    ts = min(seq, 512)
    nblk = seq // ts
    assert p_off % width == 0
    return pl.pallas_call(
        functools.partial(_pool_kernel, group=group, ts=ts),
        grid=(batch, nblk),
        in_specs=[pl.BlockSpec((ts, width), lambda b, s: (b * nblk + s, p_off // width)),
                  pl.BlockSpec((ng, group, group), lambda b, s: (0, 0, 0)),
                  pl.BlockSpec((1, width), lambda b, s: (0, 0))],
        out_specs=pl.BlockSpec((ts, width), lambda b, s: (b * nblk + s, 0)),
        out_shape=jax.ShapeDtypeStruct((T, width), BF16),
        scratch_shapes=[pltpu.VMEM((POOL_HALO, width), F32)],
        compiler_params=_params("parallel", "arbitrary"),
        name="pool",
    )(z, w_pool.astype(BF16), pool_scale.reshape(1, width).astype(F32))


def _gla_kernel(q_ref, k_ref, v_ref, r_ref, gl_ref, wg_ref, bg_ref, gain_ref, out_ref, st_ref,
                *, heads, dk, dv, chunk, nchunks):
    L = chunk

    @pl.when(pl.program_id(1) == 0)
    def _():
        st_ref[...] = jnp.zeros_like(st_ref)

    tri_b = _tril(L, BF16)
    row_id = lax.broadcasted_iota(jnp.int32, (L, 1), 0)
    lane_id = lax.broadcasted_iota(jnp.int32, (L, L), 1)
    qscale = dk ** -0.5

    def chunk_body(c, carry):
        r0 = pl.multiple_of(c * L, L)
        rows = pl.ds(r0, L)
        la = _log_sigmoid(_dot(gl_ref[rows, :].astype(BF16), wg_ref[...]) + bg_ref[...]) / GLA_TAU
        b_all = _cumsum_rows(la, tri_b)
        for h in range(heads):
            b = b_all[:, h * dk:(h + 1) * dk]
            q = q_ref[rows, h * dk:(h + 1) * dk] * qscale
            k = k_ref[rows, h * dk:(h + 1) * dk]
            vb = v_ref[rows, h * dv:(h + 1) * dv].astype(BF16)
            st = st_ref[h]

            att = jnp.zeros((L, L), F32)
            for s in range(L):
                rel = jnp.where(row_id >= s, b - b[s:s + 1, :], NEG_INF)
                col = jnp.sum(q * k[s:s + 1, :] * jnp.exp(rel), axis=-1, keepdims=True)
                att = jnp.where(lane_id == s, col, att)
            b_last = b[L - 1:L, :]
            o = _dot(att.astype(BF16), vb) + _dot((q * jnp.exp(b)).astype(BF16), st.astype(BF16), NT_DIMS)
            kdec = (k * jnp.exp(b_last - b)).astype(BF16)
            st_ref[h] = jnp.exp(b_last) * st + _dot(vb, kdec, TN_DIMS)

            y = o * lax.rsqrt(jnp.mean(o * o, axis=-1, keepdims=True) + EPS) * gain_ref[:, h * dv:(h + 1) * dv]
            r = r_ref[rows, h * dv:(h + 1) * dv]
            y = y * (r * _sigmoid(r))
            out_ref[rows, h * dv:(h + 1) * dv] = y.astype(out_ref.dtype)
        return carry

    lax.fori_loop(0, nchunks, chunk_body, 0)


def gla_block(z, w_gate, b_gate, head_gain, batch, seq, q_off, k_off, v_off, r_off, g_off):
    T = z.shape[0]
    H, L = GLA_HEADS, GLA_CHUNK
    dkt = w_gate.shape[1]
    dvt = head_gain.shape[0]
    dk, dv = dkt // H, dvt // H
    ts = min(seq, 256)
    nblk = seq // ts

    def col(off, w):
        assert off % w == 0
        return pl.BlockSpec((ts, w), lambda b, s: (b * nblk + s, off // w))

    return pl.pallas_call(
        functools.partial(_gla_kernel, heads=H, dk=dk, dv=dv, chunk=L, nchunks=ts // L),
        grid=(batch, nblk),
        in_specs=[col(q_off, dkt), col(k_off, dkt), col(v_off, dvt), col(r_off, dvt), col(g_off, LANE),
                  pl.BlockSpec((LANE, dkt), lambda b, s: (0, 0)),
                  pl.BlockSpec((1, dkt), lambda b, s: (0, 0)),
                  pl.BlockSpec((1, dvt), lambda b, s: (0, 0))],
        out_specs=pl.BlockSpec((ts, dvt), lambda b, s: (b * nblk + s, 0)),
        out_shape=jax.ShapeDtypeStruct((T, dvt), BF16),
        scratch_shapes=[pltpu.VMEM((H, dv, dk), F32)],
        compiler_params=_params("parallel", "arbitrary"),
        name="gla",
    )(z, z, z, z, z, w_gate, b_gate.reshape(1, dkt).astype(F32), head_gain.reshape(1, dvt).astype(F32))


def _extract_topk(vals, k):
    rank = jnp.full(vals.shape, float(k), F32)
    tops = []
    for i in range(k):
        mx = jnp.max(vals, axis=0, keepdims=True)
        hit = vals == mx
        rank = jnp.where(hit, float(i), rank)
        vals = jnp.where(hit, NEG_INF, vals)
        tops.append(mx)
    return jnp.concatenate(tops, axis=0), rank


def _peer_topk_kernel(qt_ref, keys_ref, cnt1_ref, p1_ref, r2_ref, p2_ref, *, heads, half, topk):
    K = topk
    for h in range(heads):
        s1 = _dot(keys_ref[h, 0], qt_ref[(2 * h) * half:(2 * h + 1) * half, :])
        s2 = _dot(keys_ref[h, 1], qt_ref[(2 * h + 1) * half:(2 * h + 2) * half, :])
        sv1, rank1 = _extract_topk(s1, K)
        sv2, rank2 = _extract_topk(s2, K)
        cand = []
        a = 0
        while K // (a + 1) > 1:
            nb = K // (a + 1)
            rows = -(-nb // SUBLANE) * SUBLANE
            c = sv1[a:a + 1, :] + sv2[:rows, :]
            if nb < rows:
                c = jnp.where(lax.broadcasted_iota(jnp.int32, (rows, 1), 0) < nb, c, NEG_INF)
            cand.append(c)
            a += 1
        n_single = a
        cand.append(sv1[n_single:, :] + sv2[0:1, :])
        work = jnp.concatenate(cand, axis=0)
        tau = None
        for _ in range(K):
            tau = jnp.max(work, axis=0, keepdims=True)
            work = jnp.where(work == tau, NEG_INF, work)
        cmax = sv1[0:1, :] + sv2[0:1, :]
        zsum = jnp.zeros_like(cmax)
        cnt1 = jnp.zeros_like(s1)
        for a, c in enumerate(cand):
            sel = c >= tau
            zsum = zsum + jnp.sum(jnp.where(sel, jnp.exp(c - cmax), 0.0), axis=0, keepdims=True)
            if a < n_single:
                cnt1 = jnp.where(rank1 == float(a), jnp.sum(sel.astype(F32), axis=0, keepdims=True), cnt1)
            else:
                for j in range(K - n_single):
                    cnt1 = jnp.where(rank1 == float(a + j), sel[j:j + 1, :].astype(F32), cnt1)
        cnt1_ref[h] = cnt1
        p1_ref[h] = jnp.exp(s1 - sv1[0:1, :]) / zsum
        r2_ref[h] = pltpu.bitcast(rank2.astype(BF16), jnp.uint32)
        p2_ref[h] = pltpu.bitcast(jnp.exp(s2 - sv2[0:1, :]).astype(BF16), jnp.uint32)


def peer_topk(qt, keys):
    heads, _, n_keys, half = keys.shape
    T = qt.shape[1]
    tt = min(T, 256)
    out_f = jax.ShapeDtypeStruct((heads, n_keys, T), F32)
    out_b = jax.ShapeDtypeStruct((heads, n_keys // 2, T), jnp.uint32)
    ospec = pl.BlockSpec((heads, n_keys, tt), lambda i: (0, 0, i))
    ospec_b = pl.BlockSpec((heads, n_keys // 2, tt), lambda i: (0, 0, i))
    return pl.pallas_call(
        functools.partial(_peer_topk_kernel, heads=heads, half=half, topk=PEER_TOPK),
        grid=(T // tt,),
        in_specs=[pl.BlockSpec((heads * 2 * half, tt), lambda i: (0, i)),
                  pl.BlockSpec((heads, 2, n_keys, half), lambda i: (0, 0, 0, 0))],
        out_specs=[ospec, ospec, ospec_b, ospec_b],
        out_shape=[out_f, out_f, out_b, out_b],
        compiler_params=_params("parallel"),
        name="peer_topk",
    )(qt, keys)


def _peer_gate_tile(a, cnt_rows, p1_rows, r2_ref, p2_ref, lanes, *, heads, n_keys):
    gelu = (0.5 * a * (1.0 + lax.erf(a * (2.0 ** -0.5)))).astype(BF16)
    gate = jnp.zeros((n_keys, LANE), BF16)
    for h in range(heads):
        cnt = jnp.broadcast_to(cnt_rows[h][:, lanes], (n_keys, LANE)).astype(BF16)
        p1 = jnp.broadcast_to(p1_rows[h][:, lanes], (n_keys, LANE)).astype(BF16)
        r2 = pltpu.bitcast(r2_ref[h, :, lanes], BF16)
        p2 = pltpu.bitcast(p2_ref[h, :, lanes], BF16)
        gate = gate + jnp.where(r2 < cnt, p2 * p1, jnp.zeros_like(gate))
    return (gate * gelu).T


def _peer_dense_step(blk, c_new, c_old, xt_ref, u_ref, v_ref, cnt1_ref, p1_ref, r2_ref, p2_ref, out_ref,
                     *, heads, n_keys, groups):
    tm = xt_ref.shape[1]
    dc = out_ref.shape[1] // groups
    for r in range(groups):
        e1 = blk * groups + r
        rows = slice(r * n_keys, (r + 1) * n_keys)
        a = _dot(u_ref[rows, :], xt_ref[...])
        cnt_rows = [cnt1_ref[h, pl.ds(e1, 1), :] for h in range(heads)]
        p1_rows = [p1_ref[h, pl.ds(e1, 1), :] for h in range(heads)]
        for lt in range(tm // LANE):
            lanes = slice(lt * LANE, (lt + 1) * LANE)
            c_new[lanes, rows] = _peer_gate_tile(a[:, lanes], cnt_rows, p1_rows, r2_ref, p2_ref, lanes,
                                                 heads=heads, n_keys=n_keys)
        cols = slice(r * dc, (r + 1) * dc)
        out_ref[:, cols] += _dot(c_old[...], v_ref[:, cols])


def _peer_dense_kernel(xt_ref, u_ref, v_ref, cnt1_ref, p1_ref, r2_ref, p2_ref, res_ref, out_ref, c0_ref, c1_ref,
                       *, heads, n_keys, groups, nblk):
    g = pl.program_id(1)
    step = functools.partial(_peer_dense_step, jnp.minimum(g, nblk - 1), xt_ref=xt_ref, u_ref=u_ref, v_ref=v_ref,
                             cnt1_ref=cnt1_ref, p1_ref=p1_ref, r2_ref=r2_ref, p2_ref=p2_ref, out_ref=out_ref,
                             heads=heads, n_keys=n_keys, groups=groups)

    @pl.when(g == 0)
    def _():
        out_ref[...] = res_ref[...]
        c1_ref[...] = jnp.zeros_like(c1_ref)

    @pl.when(lax.rem(g, 2) == 0)
    def _():
        step(c0_ref, c1_ref)

    @pl.when(lax.rem(g, 2) == 1)
    def _():
        step(c1_ref, c0_ref)


def peer_dense(xt, u, v, cnt1, p1, r2, p2, residual):
    D, T = xt.shape
    heads, n_keys, _ = cnt1.shape
    E = u.shape[0]
    groups = 4
    eb = groups * n_keys
    nblk = E // eb
    tm = min(T, 1024)
    once = pl.Buffered(1)
    gspec = pl.BlockSpec((heads, n_keys, tm), lambda i, g: (0, 0, i), pipeline_mode=once)
    gspec_b = pl.BlockSpec((heads, n_keys // 2, tm), lambda i, g: (0, 0, i), pipeline_mode=once)
    return pl.pallas_call(
        functools.partial(_peer_dense_kernel, heads=heads, n_keys=n_keys, groups=groups, nblk=nblk),
        grid=(T // tm, nblk + 1),
        in_specs=[pl.BlockSpec((D, tm), lambda i, g: (0, i), pipeline_mode=once),
                  pl.BlockSpec((eb, D), lambda i, g: (jnp.minimum(g, nblk - 1), 0)),
                  pl.BlockSpec((eb, D), lambda i, g: (jnp.maximum(g - 1, 0), 0)),
                  gspec, gspec, gspec_b, gspec_b,
                  pl.BlockSpec((tm, D), lambda i, g: (i, 0), pipeline_mode=once)],
        out_specs=pl.BlockSpec((tm, D), lambda i, g: (i, 0)),
        out_shape=jax.ShapeDtypeStruct((T, D), F32),
        scratch_shapes=[pltpu.VMEM((tm, eb), BF16), pltpu.VMEM((tm, eb), BF16)],
        compiler_params=_params("parallel", "arbitrary"),
        name="peer_dense",
    )(xt, u, v, cnt1, p1, r2, p2, residual)


def _cast_kernel(x_ref, o_ref):
    o_ref[...] = x_ref[0].astype(o_ref.dtype)


def cast_table(tabs, layer, dtype):
    _, rows, cols = tabs.shape
    tr = min(rows, 1024)
    return pl.pallas_call(
        _cast_kernel,
        grid=(rows // tr,),
        in_specs=[pl.BlockSpec((1, tr, cols), lambda i: (layer, i, 0))],
        out_specs=pl.BlockSpec((tr, cols), lambda i: (i, 0)),
        out_shape=jax.ShapeDtypeStruct((rows, cols), dtype),
        compiler_params=_params("parallel"),
        name="cast_table",
    )(tabs)


def peer_layer(x, norm_g, w_q, sub_keys, u_tabs, v_tabs, layer):
    ht = rmsnorm(x, norm_g, BF16, transpose=True)
    qt = matmul(w_q.T.astype(BF16), ht, BF16, tn=512, name="peer_q")
    cnt1, p1, r2, p2 = peer_topk(qt, sub_keys.astype(BF16))
    return peer_dense(ht, cast_table(u_tabs, layer, BF16), cast_table(v_tabs, layer, BF16), cnt1, p1, r2, p2, x)


def _pad_cols(w, n):
    return jnp.pad(w, ((0, 0), (0, n - w.shape[1])))


def even_layer(x, batch, seq, norm_g, w_in, b_igate, b_fgate, head_gain, w_pool, pool_scale, w_out):
    H = MLSTM_HEADS
    width = head_gain.shape[0]
    qk = width // 2
    pool_w = pool_scale.shape[0]
    n_main = 2 * qk + 2 * width
    h = rmsnorm(x, norm_g, BF16)
    w_cat = jnp.concatenate([w_in[:, :n_main], w_in[:, n_main + 2 * H:], w_in[:, n_main:n_main + 2 * H]], axis=1)
    tn = 3 * MXU_COLS
    z = matmul(h, _pad_cols(w_cat, -(-w_cat.shape[1] // tn) * tn).astype(BF16), F32, tn=tn, name="even_in")
    gate_bias = jnp.pad(jnp.concatenate([b_igate, b_fgate]), (0, LANE - 2 * H)).reshape(1, LANE).astype(F32)
    hm = mlstm_block(z, gate_bias, head_gain, batch, seq, 0, qk, 2 * qk, 2 * qk + width, n_main + pool_w)
    hp = pool_block(z, w_pool, pool_scale, batch, seq, n_main)
    y = jnp.concatenate([hm, hp], axis=1)
    return matmul(y, w_out.astype(BF16), F32, tn=4 * MXU_COLS, residual=x, name="even_out")


def odd_layer(x, batch, seq, norm_g, w_in, w_gate, b_gate, head_gain, w_out):
    rank, dkt = w_gate.shape
    dvt = head_gain.shape[0]
    n_main = 2 * dkt + 2 * dvt
    h = rmsnorm(x, norm_g, BF16)
    tn = 5 * MXU_COLS
    z = matmul(h, _pad_cols(w_in, -(-w_in.shape[1] // tn) * tn).astype(BF16), F32, tn=tn, name="odd_in")
    wg = jnp.pad(w_gate, ((0, LANE - rank), (0, 0))).astype(BF16)
    y = gla_block(z, wg, b_gate, head_gain, batch, seq, 0, dkt, 2 * dkt, 2 * dkt + dvt, n_main)
    return matmul(y, w_out.astype(BF16), F32, tn=4 * MXU_COLS, residual=x, name="odd_out")


def kernel(x, e_norm, e_w_in, e_b_igate, e_b_fgate, e_head_gain, e_w_pool, e_pool_scale, e_w_out, o_norm, o_w_in, o_w_gate, o_b_gate, o_head_gain, o_w_out, f_norm, f_w_q, f_sub_keys, f_u, f_v, final_norm):
    B, S, D = x.shape
    depth = f_norm.shape[0]
    h = x.reshape(B * S, D)
    for layer in range(depth):
        j = layer // 2
        if layer % 2 == 0:
            h = even_layer(h, B, S, e_norm[j], e_w_in[j], e_b_igate[j], e_b_fgate[j], e_head_gain[j],
                           e_w_pool[j], e_pool_scale[j], e_w_out[j])
        else:
            h = odd_layer(h, B, S, o_norm[j], o_w_in[j], o_w_gate[j], o_b_gate[j], o_head_gain[j], o_w_out[j])
        h = peer_layer(h, f_norm[layer], f_w_q[layer], f_sub_keys[layer], f_u, f_v, layer)
    return rmsnorm(h, final_norm, F32).reshape(B, S, D)
```

```python
import functools

import jax
import jax.numpy as jnp
from jax import lax
from jax.experimental import pallas as pl
from jax.experimental.pallas import tpu as pltpu

F32 = jnp.float32
BF16 = jnp.bfloat16
EPS = 1e-6
LANE = 128
SUBLANE = 8
MXU_COLS = 256
VMEM_LIMIT = 56 * 1024 * 1024

MLSTM_HEADS = 4
MLSTM_CHUNK = 64
POOL_WINDOWS = (2, 4, 8, 16)
POOL_HALO = 16
GLA_HEADS = 4
GLA_CHUNK = 32
GLA_TAU = 16.0
PEER_TOPK = 16
NEG_INF = float("-inf")
LOG2E = 1.4426950408889634

NT_DIMS = (((1,), (1,)), ((), ()))
TN_DIMS = (((0,), (0,)), ((), ()))


def _params(*sem):
    return pltpu.CompilerParams(dimension_semantics=sem, vmem_limit_bytes=VMEM_LIMIT)


def _log_sigmoid(x):
    return jnp.minimum(x, 0.0) - jnp.log1p(jnp.exp(-jnp.abs(x)))


def _sigmoid(x):
    return 1.0 / (1.0 + jnp.exp(-x))


def _dot(a, b, dims=None):
    if dims is None:
        return jnp.dot(a, b, preferred_element_type=F32)
    return lax.dot_general(a, b, dims, preferred_element_type=F32)


def _cumsum_rows(x, tri):
    hi = x.astype(BF16)
    lo = (x - hi.astype(F32)).astype(BF16)
    return _dot(tri, hi) + _dot(tri, lo)


def _tril(n, dtype):
    r = lax.broadcasted_iota(jnp.int32, (n, n), 0)
    c = lax.broadcasted_iota(jnp.int32, (n, n), 1)
    return (c <= r).astype(dtype)


def _rms_kernel(x_ref, g_ref, o_ref, *, transpose):
    x = x_ref[...]
    y = x * lax.rsqrt(jnp.mean(x * x, axis=-1, keepdims=True) + EPS) * g_ref[...]
    if transpose:
        y = y.T
    o_ref[...] = y.astype(o_ref.dtype)


def rmsnorm(x, g, out_dtype, transpose=False):
    T, D = x.shape
    tm = min(T, 512)
    if transpose:
        out_shape, out_spec = (D, T), pl.BlockSpec((D, tm), lambda i: (0, i))
    else:
        out_shape, out_spec = (T, D), pl.BlockSpec((tm, D), lambda i: (i, 0))
    return pl.pallas_call(
        functools.partial(_rms_kernel, transpose=transpose),
        grid=(T // tm,),
        in_specs=[pl.BlockSpec((tm, D), lambda i: (i, 0)), pl.BlockSpec((1, D), lambda i: (0, 0))],
        out_specs=out_spec,
        out_shape=jax.ShapeDtypeStruct(out_shape, out_dtype),
        compiler_params=_params("parallel"),
        name="rmsnorm_t" if transpose else "rmsnorm",
    )(x, g.reshape(1, D).astype(F32))


def _mm_kernel(a_ref, b_ref, o_ref):
    o_ref[...] = _dot(a_ref[...], b_ref[...]).astype(o_ref.dtype)


def _mm_res_kernel(a_ref, b_ref, r_ref, o_ref):
    o_ref[...] = (_dot(a_ref[...], b_ref[...]) + r_ref[...]).astype(o_ref.dtype)


def matmul(a, b, out_dtype, tn, residual=None, name="matmul"):
    M, K = a.shape
    N = b.shape[1]
    tm = min(M, 1024)
    tn = min(N, tn)
    in_specs = [pl.BlockSpec((tm, K), lambda i, j: (i, 0)), pl.BlockSpec((K, tn), lambda i, j: (0, j))]
    args = [a, b]
    body = _mm_kernel
    if residual is not None:
        in_specs.append(pl.BlockSpec((tm, tn), lambda i, j: (i, j)))
        args.append(residual)
        body = _mm_res_kernel
    return pl.pallas_call(
        body,
        grid=(M // tm, N // tn),
        in_specs=in_specs,
        out_specs=pl.BlockSpec((tm, tn), lambda i, j: (i, j)),
        out_shape=jax.ShapeDtypeStruct((M, N), out_dtype),
        compiler_params=_params("parallel", "parallel"),
        name=name,
    )(*args)


def _mlstm_kernel(q_ref, k_ref, v_ref, o_ref, g_ref, gb_ref, gain_ref, out_ref, cn_ref, m_ref,
                  *, batch, heads, dk, dv, chunk, nchunks):
    L = chunk

    @pl.when(pl.program_id(0) == 0)
    def _():
        cn_ref[...] = jnp.zeros_like(cn_ref)
        m_ref[...] = jnp.zeros_like(m_ref)

    tri_b = _tril(L, BF16)
    causal = _tril(L, jnp.int32) > 0
    ones_col = (lax.broadcasted_iota(jnp.int32, (L, LANE), 1) == 0).astype(F32)
    kscale = dk ** -0.5

    chains = [(b, h) for b in range(batch) for h in range(heads)]
    n_chain = len(chains)

    def per_chain(fn):
        return jnp.stack([fn(b, h) for b, h in chains], axis=0)

    def chunk_body(c, carry):
        r0 = pl.multiple_of(c * L, L)
        rows = pl.ds(r0, L)
        g = [g_ref[b, rows, :] + gb_ref[...] for b in range(batch)]
        bc = [_cumsum_rows(_log_sigmoid(g[b]), tri_b) for b in range(batch)]
        g_t = [x.T for x in g]
        bc_t = [x.T for x in bc]
        i_col = per_chain(lambda b, h: g[b][:, h:h + 1])
        i_row = per_chain(lambda b, h: g_t[b][h:h + 1, :])
        b_col = per_chain(lambda b, h: bc[b][:, heads + h:heads + h + 1])
        b_row = per_chain(lambda b, h: bc_t[b][heads + h:heads + h + 1, :])
        m_prev = m_ref[:, 0:1, 0:1]

        dmat = jnp.where(causal, b_col - b_row + i_row, NEG_INF)
        inter = b_col + m_prev
        m_t = jnp.maximum(jnp.max(dmat, axis=-1, keepdims=True), inter)
        a_inter = jnp.exp(inter - m_t)
        decay = jnp.exp(dmat - m_t)
        b_last = b_col[:, L - 1:L, :]
        dec = b_last - b_col + i_col
        m_new = jnp.maximum(b_last + m_prev, jnp.max(dec, axis=1, keepdims=True))
        wk = jnp.exp(dec - m_new)
        sc = jnp.exp(b_last + m_prev - m_new)

        qb = [q_ref[b, rows, h * dk:(h + 1) * dk].astype(BF16) for b, h in chains]
        kb = [(k_ref[b, rows, h * dk:(h + 1) * dk] * kscale).astype(BF16) for b, h in chains]
        v_aug = per_chain(lambda b, h: jnp.concatenate([v_ref[b, rows, h * dv:(h + 1) * dv], ones_col], axis=1))
        v_aug_b = v_aug.astype(BF16)
        s = (jnp.stack([_dot(qb[i], kb[i], NT_DIMS) for i in range(n_chain)], axis=0) * decay).astype(BF16)
        intra = jnp.stack([_dot(s[i], v_aug_b[i]) for i in range(n_chain)], axis=0)
        inter_out = jnp.stack([_dot(qb[i], cn_ref[i].astype(BF16), NT_DIMS) for i in range(n_chain)], axis=0)
        hn = intra + a_inter * inter_out

        wv = (wk * v_aug).astype(BF16)
        cn_ref[...] = sc * cn_ref[...] + jnp.stack([_dot(wv[i], kb[i], TN_DIMS) for i in range(n_chain)], axis=0)
        m_ref[...] = jnp.broadcast_to(m_new, m_ref.shape)

        hh = hn[:, :, :dv] / jnp.maximum(jnp.abs(hn[:, :, dv:dv + 1]), jnp.exp(-m_t))
        gain = per_chain(lambda b, h: gain_ref[:, h * dv:(h + 1) * dv])
        o_gate = _sigmoid(per_chain(lambda b, h: o_ref[b, rows, h * dv:(h + 1) * dv]))
        y = (hh * lax.rsqrt(jnp.mean(hh * hh, axis=-1, keepdims=True) + EPS) * gain * o_gate).astype(out_ref.dtype)
        for i, (b, h) in enumerate(chains):
            out_ref[b, rows, h * dv:(h + 1) * dv] = y[i]
        return carry

    lax.fori_loop(0, nchunks, chunk_body, 0)


def mlstm_block(z, gate_bias, head_gain, batch, seq, q_off, k_off, v_off, o_off, g_off):
    T, cols = z.shape
    H, L = MLSTM_HEADS, MLSTM_CHUNK
    width = head_gain.shape[0]
    dv = width // H
    dk = dv // 2
    ts = min(seq, 256)
    z3 = z.reshape(batch, seq, cols)

    def col(off, w):
        assert off % w == 0
        return pl.BlockSpec((batch, ts, w), lambda s: (0, s, off // w))

    n_state = batch * H
    out = pl.pallas_call(
        functools.partial(_mlstm_kernel, batch=batch, heads=H, dk=dk, dv=dv, chunk=L, nchunks=ts // L),
        grid=(seq // ts,),
        in_specs=[col(q_off, H * dk), col(k_off, H * dk), col(v_off, width), col(o_off, width), col(g_off, LANE),
                  pl.BlockSpec((1, LANE), lambda s: (0, 0)), pl.BlockSpec((1, width), lambda s: (0, 0))],
        out_specs=pl.BlockSpec((batch, ts, width), lambda s: (0, s, 0)),
        out_shape=jax.ShapeDtypeStruct((batch, seq, width), BF16),
        scratch_shapes=[pltpu.VMEM((n_state, dv + LANE, dk), F32),
                        pltpu.VMEM((n_state, SUBLANE, LANE), F32)],
        compiler_params=_params("arbitrary"),
        name="mlstm",
    )(z3, z3, z3, z3, z3, gate_bias, head_gain.reshape(1, width).astype(F32))
    return out.reshape(T, width)


def _pool_kernel(p_ref, w_ref, s_ref, out_ref, halo_ref, *, group, ts):
    si = pl.program_id(1)

    @pl.when(si == 0)
    def _():
        halo_ref[...] = jnp.zeros_like(halo_ref)

    p = p_ref[...]
    buf = jnp.concatenate([halo_ref[...], p], axis=0)
    t_abs = si * ts + lax.broadcasted_iota(jnp.int32, (ts, 1), 0)
    win = buf
    span = 1
    for g, w in enumerate(POOL_WINDOWS):
        while span < w:
            win = win + pltpu.roll(win, span, 0)
            span *= 2
        sl = slice(g * group, (g + 1) * group)
        cnt = jnp.minimum(t_abs + 1, w).astype(F32)
        mixed = win[POOL_HALO:, sl] / cnt - p[:, sl]
        y = _dot(mixed.astype(BF16), w_ref[g]) * s_ref[:, sl]
        out_ref[:, sl] = y.astype(out_ref.dtype)
    halo_ref[...] = p[ts - POOL_HALO:, :]


def pool_block(z, w_pool, pool_scale, batch, seq, p_off):
    T = z.shape[0]
    ng, group, _ = w_pool.shape
    width = ng * group
    ts = min(seq, 512)
    nblk = seq // ts
    assert p_off % width == 0
    return pl.pallas_call(
        functools.partial(_pool_kernel, group=group, ts=ts),
        grid=(batch, nblk),
        in_specs=[pl.BlockSpec((ts, width), lambda b, s: (b * nblk + s, p_off // width)),
                  pl.BlockSpec((ng, group, group), lambda b, s: (0, 0, 0)),
                  pl.BlockSpec((1, width), lambda b, s: (0, 0))],
        out_specs=pl.BlockSpec((ts, width), lambda b, s: (b * nblk + s, 0)),
        out_shape=jax.ShapeDtypeStruct((T, width), BF16),
        scratch_shapes=[pltpu.VMEM((POOL_HALO, width), F32)],
        compiler_params=_params("parallel", "arbitrary"),
        name="pool",
    )(z, w_pool.astype(BF16), pool_scale.reshape(1, width).astype(F32))


def _gla_kernel(q_ref, k_ref, v_ref, r_ref, gl_ref, wg_ref, bg_ref, gain_ref, out_ref, st_ref,
                *, batch, heads, dk, dv, chunk, nchunks):
    L = chunk
    Lh = L // 2

    @pl.when(pl.program_id(0) == 0)
    def _():
        st_ref[...] = jnp.zeros_like(st_ref)

    tri_b = _tril(L, BF16)
    row_id = lax.broadcasted_iota(jnp.int32, (L, 1), 0)
    row_half = lax.broadcasted_iota(jnp.int32, (Lh, 1), 0)
    lane_id = lax.broadcasted_iota(jnp.int32, (Lh, L), 1)
    qscale = dk ** -0.5
    chains = [(b, h) for b in range(batch) for h in range(heads)]
    n_chain = len(chains)

    def per_chain(fn):
        return jnp.stack([fn(b, h) for b, h in chains], axis=0)

    def chunk_body(c, carry):
        r0 = pl.multiple_of(c * L, L)
        rows = pl.ds(r0, L)
        b_seq = []
        for b in range(batch):
            la = _log_sigmoid(_dot(gl_ref[b, rows, :].astype(BF16), wg_ref[...]) + bg_ref[...]) / GLA_TAU
            b_seq.append(_cumsum_rows(la, tri_b))
        bb = per_chain(lambda b, h: b_seq[b][:, h * dk:(h + 1) * dk]) * LOG2E
        q = per_chain(lambda b, h: q_ref[b, rows, h * dk:(h + 1) * dk]) * qscale
        k = per_chain(lambda b, h: k_ref[b, rows, h * dk:(h + 1) * dk])

        att_half = [jnp.zeros((n_chain, Lh, L), F32), jnp.zeros((n_chain, Lh, L), F32)]
        for s in range(L):
            hf = s // Lh
            h0 = hf * Lh
            rel = jnp.where(row_half >= s - h0, bb[:, h0:h0 + Lh, :] - bb[:, s:s + 1, :], NEG_INF)
            col = jnp.sum(q[:, h0:h0 + Lh, :] * k[:, s:s + 1, :] * jnp.exp2(rel), axis=-1, keepdims=True)
            att_half[hf] = jnp.where(lane_id == s, col, att_half[hf])
        anchor = bb[:, Lh - 1:Lh, :]
        qd = (q * jnp.exp2(jnp.where(row_id >= Lh, bb - anchor, NEG_INF))).astype(BF16)
        kd = (k * jnp.exp2(jnp.where(row_id < Lh, anchor - bb, NEG_INF))).astype(BF16)
        att = jnp.concatenate(att_half, axis=1) + jnp.stack([_dot(qd[i], kd[i], NT_DIMS) for i in range(n_chain)], axis=0)
        att = att.astype(BF16)

        b_last = bb[:, L - 1:L, :]
        q_in = (q * jnp.exp2(bb)).astype(BF16)
        k_out = (k * jnp.exp2(b_last - bb)).astype(BF16)
        vb = [v_ref[b, rows, h * dv:(h + 1) * dv].astype(BF16) for b, h in chains]
        o = jnp.stack([_dot(att[i], vb[i]) + _dot(q_in[i], st_ref[i].astype(BF16), NT_DIMS) for i in range(n_chain)], axis=0)
        st_ref[...] = jnp.exp2(b_last) * st_ref[...] + jnp.stack([_dot(vb[i], k_out[i], TN_DIMS) for i in range(n_chain)], axis=0)

        gain = per_chain(lambda b, h: gain_ref[:, h * dv:(h + 1) * dv])
        r = per_chain(lambda b, h: r_ref[b, rows, h * dv:(h + 1) * dv])
        y = o * lax.rsqrt(jnp.mean(o * o, axis=-1, keepdims=True) + EPS) * gain * (r * _sigmoid(r))
        y = y.astype(out_ref.dtype)
        for i, (b, h) in enumerate(chains):
            out_ref[b, rows, h * dv:(h + 1) * dv] = y[i]
        return carry

    lax.fori_loop(0, nchunks, chunk_body, 0)


def gla_block(z, w_gate, b_gate, head_gain, batch, seq, q_off, k_off, v_off, r_off, g_off):
    T, cols = z.shape
    H, L = GLA_HEADS, GLA_CHUNK
    dkt = w_gate.shape[1]
    dvt = head_gain.shape[0]
    dk, dv = dkt // H, dvt // H
    ts = min(seq, 128)
    z3 = z.reshape(batch, seq, cols)

    def col(off, w):
        assert off % w == 0
        return pl.BlockSpec((batch, ts, w), lambda s: (0, s, off // w))

    out = pl.pallas_call(
        functools.partial(_gla_kernel, batch=batch, heads=H, dk=dk, dv=dv, chunk=L, nchunks=ts // L),
        grid=(seq // ts,),
        in_specs=[col(q_off, dkt), col(k_off, dkt), col(v_off, dvt), col(r_off, dvt), col(g_off, LANE),
                  pl.BlockSpec((LANE, dkt), lambda s: (0, 0)),
                  pl.BlockSpec((1, dkt), lambda s: (0, 0)),
                  pl.BlockSpec((1, dvt), lambda s: (0, 0))],
        out_specs=pl.BlockSpec((batch, ts, dvt), lambda s: (0, s, 0)),
        out_shape=jax.ShapeDtypeStruct((batch, seq, dvt), BF16),
        scratch_shapes=[pltpu.VMEM((batch * H, dv, dk), F32)],
        compiler_params=_params("arbitrary"),
        name="gla",
    )(z3, z3, z3, z3, z3, w_gate, b_gate.reshape(1, dkt).astype(F32), head_gain.reshape(1, dvt).astype(F32))
    return out.reshape(T, dvt)


def _extract_topk(vals, k):
    rank = jnp.full(vals.shape, float(k), F32)
    tops = []
    for i in range(k):
        mx = jnp.max(vals, axis=0, keepdims=True)
        hit = vals == mx
        rank = jnp.where(hit, float(i), rank)
        vals = jnp.where(hit, NEG_INF, vals)
        tops.append(mx)
    return jnp.concatenate(tops, axis=0), rank


def _peer_topk_kernel(qt_ref, keys_ref, cnt1_ref, p1_ref, r2_ref, p2_ref, *, heads, half, topk):
    K = topk
    for h in range(heads):
        s1 = _dot(keys_ref[h, 0], qt_ref[(2 * h) * half:(2 * h + 1) * half, :])
        s2 = _dot(keys_ref[h, 1], qt_ref[(2 * h + 1) * half:(2 * h + 2) * half, :])
        sv1, rank1 = _extract_topk(s1, K)
        sv2, rank2 = _extract_topk(s2, K)
        cand = []
        a = 0
        while K // (a + 1) > 1:
            nb = K // (a + 1)
            rows = -(-nb // SUBLANE) * SUBLANE
            c = sv1[a:a + 1, :] + sv2[:rows, :]
            if nb < rows:
                c = jnp.where(lax.broadcasted_iota(jnp.int32, (rows, 1), 0) < nb, c, NEG_INF)
            cand.append(c)
            a += 1
        n_single = a
        cand.append(sv1[n_single:, :] + sv2[0:1, :])
        work = jnp.concatenate(cand, axis=0)
        tau = None
        for _ in range(K):
            tau = jnp.max(work, axis=0, keepdims=True)
            work = jnp.where(work == tau, NEG_INF, work)
        cmax = sv1[0:1, :] + sv2[0:1, :]
        zsum = jnp.zeros_like(cmax)
        cnt1 = jnp.zeros_like(s1)
        for a, c in enumerate(cand):
            sel = c >= tau
            zsum = zsum + jnp.sum(jnp.where(sel, jnp.exp(c - cmax), 0.0), axis=0, keepdims=True)
            if a < n_single:
                cnt1 = jnp.where(rank1 == float(a), jnp.sum(sel.astype(F32), axis=0, keepdims=True), cnt1)
            else:
                for j in range(K - n_single):
                    cnt1 = jnp.where(rank1 == float(a + j), sel[j:j + 1, :].astype(F32), cnt1)
        cnt1_ref[h] = cnt1
        p1_ref[h] = jnp.exp(s1 - sv1[0:1, :]) / zsum
        r2_ref[h] = pltpu.bitcast(rank2.astype(BF16), jnp.uint32)
        p2_ref[h] = pltpu.bitcast(jnp.exp(s2 - sv2[0:1, :]).astype(BF16), jnp.uint32)


def peer_topk(qt, keys):
    heads, _, n_keys, half = keys.shape
    T = qt.shape[1]
    tt = min(T, 256)
    out_f = jax.ShapeDtypeStruct((heads, n_keys, T), F32)
    out_b = jax.ShapeDtypeStruct((heads, n_keys // 2, T), jnp.uint32)
    ospec = pl.BlockSpec((heads, n_keys, tt), lambda i: (0, 0, i))
    ospec_b = pl.BlockSpec((heads, n_keys // 2, tt), lambda i: (0, 0, i))
    return pl.pallas_call(
        functools.partial(_peer_topk_kernel, heads=heads, half=half, topk=PEER_TOPK),
        grid=(T // tt,),
        in_specs=[pl.BlockSpec((heads * 2 * half, tt), lambda i: (0, i)),
                  pl.BlockSpec((heads, 2, n_keys, half), lambda i: (0, 0, 0, 0))],
        out_specs=[ospec, ospec, ospec_b, ospec_b],
        out_shape=[out_f, out_f, out_b, out_b],
        compiler_params=_params("parallel"),
        name="peer_topk",
    )(qt, keys)


def _peer_gate_tile(a, cnt_rows, p1_rows, r2_ref, p2_ref, lanes, *, heads, n_keys):
    gelu = (0.5 * a * (1.0 + lax.erf(a * (2.0 ** -0.5)))).astype(BF16)
    gate = jnp.zeros((n_keys, LANE), BF16)
    for h in range(heads):
        cnt = jnp.broadcast_to(cnt_rows[h][:, lanes], (n_keys, LANE)).astype(BF16)
        p1 = jnp.broadcast_to(p1_rows[h][:, lanes], (n_keys, LANE)).astype(BF16)
        r2 = pltpu.bitcast(r2_ref[h, :, lanes], BF16)
        p2 = pltpu.bitcast(p2_ref[h, :, lanes], BF16)
        gate = gate + jnp.where(r2 < cnt, p2 * p1, jnp.zeros_like(gate))
    return (gate * gelu).T


def _peer_dense_step(blk, c_new, c_old, xt_ref, u_ref, v_ref, cnt1_ref, p1_ref, r2_ref, p2_ref, out_ref,
                     *, heads, n_keys, groups):
    tm = xt_ref.shape[1]
    dc = out_ref.shape[1] // groups
    for r in range(groups):
        e1 = blk * groups + r
        rows = slice(r * n_keys, (r + 1) * n_keys)
        a = _dot(u_ref[rows, :], xt_ref[...])
        cnt_rows = [cnt1_ref[h, pl.ds(e1, 1), :] for h in range(heads)]
        p1_rows = [p1_ref[h, pl.ds(e1, 1), :] for h in range(heads)]
        for lt in range(tm // LANE):
            lanes = slice(lt * LANE, (lt + 1) * LANE)
            c_new[lanes, rows] = _peer_gate_tile(a[:, lanes], cnt_rows, p1_rows, r2_ref, p2_ref, lanes,
                                                 heads=heads, n_keys=n_keys)
        cols = slice(r * dc, (r + 1) * dc)
        out_ref[:, cols] += _dot(c_old[...], v_ref[:, cols])


def _peer_dense_kernel(xt_ref, u_ref, v_ref, cnt1_ref, p1_ref, r2_ref, p2_ref, res_ref, out_ref, c0_ref, c1_ref,
                       *, heads, n_keys, groups, nblk):
    g = pl.program_id(1)
    step = functools.partial(_peer_dense_step, jnp.minimum(g, nblk - 1), xt_ref=xt_ref, u_ref=u_ref, v_ref=v_ref,
                             cnt1_ref=cnt1_ref, p1_ref=p1_ref, r2_ref=r2_ref, p2_ref=p2_ref, out_ref=out_ref,
                             heads=heads, n_keys=n_keys, groups=groups)

    @pl.when(g == 0)
    def _():
        out_ref[...] = res_ref[...]
        c1_ref[...] = jnp.zeros_like(c1_ref)

    @pl.when(lax.rem(g, 2) == 0)
    def _():
        step(c0_ref, c1_ref)

    @pl.when(lax.rem(g, 2) == 1)
    def _():
        step(c1_ref, c0_ref)


def peer_dense(xt, u, v, cnt1, p1, r2, p2, residual):
    D, T = xt.shape
    heads, n_keys, _ = cnt1.shape
    E = u.shape[0]
    groups = 4
    eb = groups * n_keys
    nblk = E // eb
    tm = min(T, 512)
    gspec = pl.BlockSpec((heads, n_keys, tm), lambda i, g: (0, 0, i))
    gspec_b = pl.BlockSpec((heads, n_keys // 2, tm), lambda i, g: (0, 0, i))
    return pl.pallas_call(
        functools.partial(_peer_dense_kernel, heads=heads, n_keys=n_keys, groups=groups, nblk=nblk),
        grid=(T // tm, nblk + 1),
        in_specs=[pl.BlockSpec((D, tm), lambda i, g: (0, i)),
                  pl.BlockSpec((eb, D), lambda i, g: (jnp.minimum(g, nblk - 1), 0)),
                  pl.BlockSpec((eb, D), lambda i, g: (jnp.maximum(g - 1, 0), 0)),
                  gspec, gspec, gspec_b, gspec_b,
                  pl.BlockSpec((tm, D), lambda i, g: (i, 0), pipeline_mode=pl.Buffered(1))],
        out_specs=pl.BlockSpec((tm, D), lambda i, g: (i, 0)),
        out_shape=jax.ShapeDtypeStruct((T, D), F32),
        scratch_shapes=[pltpu.VMEM((tm, eb), BF16), pltpu.VMEM((tm, eb), BF16)],
        compiler_params=_params("parallel", "arbitrary"),
        name="peer_dense",
    )(xt, u, v, cnt1, p1, r2, p2, residual)


def _cast_kernel(x_ref, o_ref):
    o_ref[...] = x_ref[0].astype(o_ref.dtype)


def cast_table(tabs, layer, dtype):
    _, rows, cols = tabs.shape
    tr = min(rows, 1024)
    return pl.pallas_call(
        _cast_kernel,
        grid=(rows // tr,),
        in_specs=[pl.BlockSpec((1, tr, cols), lambda i: (layer, i, 0))],
        out_specs=pl.BlockSpec((tr, cols), lambda i: (i, 0)),
        out_shape=jax.ShapeDtypeStruct((rows, cols), dtype),
        compiler_params=_params("parallel"),
        name="cast_table",
    )(tabs)


def peer_layer(x, norm_g, w_q, sub_keys, u_tabs, v_tabs, layer):
    ht = rmsnorm(x, norm_g, BF16, transpose=True)
    qt = matmul(w_q.T.astype(BF16), ht, BF16, tn=512, name="peer_q")
    cnt1, p1, r2, p2 = peer_topk(qt, sub_keys.astype(BF16))
    return peer_dense(ht, cast_table(u_tabs, layer, BF16), cast_table(v_tabs, layer, BF16), cnt1, p1, r2, p2, x)


def _pad_cols(w, n):
    return jnp.pad(w, ((0, 0), (0, n - w.shape[1])))


def even_layer(x, batch, seq, norm_g, w_in, b_igate, b_fgate, head_gain, w_pool, pool_scale, w_out):
    H = MLSTM_HEADS
    width = head_gain.shape[0]
    qk = width // 2
    pool_w = pool_scale.shape[0]
    n_main = 2 * qk + 2 * width
    h = rmsnorm(x, norm_g, BF16)
    w_cat = jnp.concatenate([w_in[:, :n_main], w_in[:, n_main + 2 * H:], w_in[:, n_main:n_main + 2 * H]], axis=1)
    tn = 3 * MXU_COLS
    z = matmul(h, _pad_cols(w_cat, -(-w_cat.shape[1] // tn) * tn).astype(BF16), F32, tn=tn, name="even_in")
    gate_bias = jnp.pad(jnp.concatenate([b_igate, b_fgate]), (0, LANE - 2 * H)).reshape(1, LANE).astype(F32)
    hm = mlstm_block(z, gate_bias, head_gain, batch, seq, 0, qk, 2 * qk, 2 * qk + width, n_main + pool_w)
    hp = pool_block(z, w_pool, pool_scale, batch, seq, n_main)
    y = jnp.concatenate([hm, hp], axis=1)
    return matmul(y, w_out.astype(BF16), F32, tn=4 * MXU_COLS, residual=x, name="even_out")


def odd_layer(x, batch, seq, norm_g, w_in, w_gate, b_gate, head_gain, w_out):
    rank, dkt = w_gate.shape
    dvt = head_gain.shape[0]
    n_main = 2 * dkt + 2 * dvt
    h = rmsnorm(x, norm_g, BF16)
    tn = 5 * MXU_COLS
    z = matmul(h, _pad_cols(w_in, -(-w_in.shape[1] // tn) * tn).astype(BF16), F32, tn=tn, name="odd_in")
    wg = jnp.pad(w_gate, ((0, LANE - rank), (0, 0))).astype(BF16)
    y = gla_block(z, wg, b_gate, head_gain, batch, seq, 0, dkt, 2 * dkt, 2 * dkt + dvt, n_main)
    return matmul(y, w_out.astype(BF16), F32, tn=4 * MXU_COLS, residual=x, name="odd_out")


def kernel(x, e_norm, e_w_in, e_b_igate, e_b_fgate, e_head_gain, e_w_pool, e_pool_scale, e_w_out, o_norm, o_w_in, o_w_gate, o_b_gate, o_head_gain, o_w_out, f_norm, f_w_q, f_sub_keys, f_u, f_v, final_norm):
    B, S, D = x.shape
    depth = f_norm.shape[0]
    h = x.reshape(B * S, D)
    for layer in range(depth):
        j = layer // 2
        if layer % 2 == 0:
            h = even_layer(h, B, S, e_norm[j], e_w_in[j], e_b_igate[j], e_b_fgate[j], e_head_gain[j],
                           e_w_pool[j], e_pool_scale[j], e_w_out[j])
        else:
            h = odd_layer(h, B, S, o_norm[j], o_w_in[j], o_w_gate[j], o_b_gate[j], o_head_gain[j], o_w_out[j])
        h = peer_layer(h, f_norm[layer], f_w_q[layer], f_sub_keys[layer], f_u, f_v, layer)
    return rmsnorm(h, final_norm, F32).reshape(B, S, D)
```

```python
import functools

import jax
import jax.numpy as jnp
from jax import lax
from jax.experimental import pallas as pl
from jax.experimental.pallas import tpu as pltpu

F32 = jnp.float32
BF16 = jnp.bfloat16
EPS = 1e-6
LANE = 128
SUBLANE = 8
MXU_COLS = 256
VMEM_LIMIT = 56 * 1024 * 1024

MLSTM_HEADS = 4
MLSTM_CHUNK = 64
POOL_WINDOWS = (2, 4, 8, 16)
POOL_HALO = 16
GLA_HEADS = 4
GLA_CHUNK = 32
GLA_TAU = 16.0
PEER_TOPK = 16
NEG_INF = float("-inf")
LOG2E = 1.4426950408889634

NT_DIMS = (((1,), (1,)), ((), ()))
TN_DIMS = (((0,), (0,)), ((), ()))


def _params(*sem):
    return pltpu.CompilerParams(dimension_semantics=sem, vmem_limit_bytes=VMEM_LIMIT)


def _log_sigmoid(x):
    return jnp.minimum(x, 0.0) - jnp.log1p(jnp.exp(-jnp.abs(x)))


def _sigmoid(x):
    return 1.0 / (1.0 + jnp.exp(-x))


def _dot(a, b, dims=None):
    if dims is None:
        return jnp.dot(a, b, preferred_element_type=F32)
    return lax.dot_general(a, b, dims, preferred_element_type=F32)


def _cumsum_rows(x, tri):
    hi = x.astype(BF16)
    lo = (x - hi.astype(F32)).astype(BF16)
    return _dot(tri, hi) + _dot(tri, lo)


def _tril(n, dtype):
    r = lax.broadcasted_iota(jnp.int32, (n, n), 0)
    c = lax.broadcasted_iota(jnp.int32, (n, n), 1)
    return (c <= r).astype(dtype)


def _rms_kernel(x_ref, g_ref, o_ref, *, transpose):
    x = x_ref[...]
    y = x * lax.rsqrt(jnp.mean(x * x, axis=-1, keepdims=True) + EPS) * g_ref[...]
    if transpose:
        y = y.T
    o_ref[...] = y.astype(o_ref.dtype)


def rmsnorm(x, g, out_dtype, transpose=False):
    T, D = x.shape
    tm = min(T, 512)
    if transpose:
        out_shape, out_spec = (D, T), pl.BlockSpec((D, tm), lambda i: (0, i))
    else:
        out_shape, out_spec = (T, D), pl.BlockSpec((tm, D), lambda i: (i, 0))
    return pl.pallas_call(
        functools.partial(_rms_kernel, transpose=transpose),
        grid=(T // tm,),
        in_specs=[pl.BlockSpec((tm, D), lambda i: (i, 0)), pl.BlockSpec((1, D), lambda i: (0, 0))],
        out_specs=out_spec,
        out_shape=jax.ShapeDtypeStruct(out_shape, out_dtype),
        compiler_params=_params("parallel"),
        name="rmsnorm_t" if transpose else "rmsnorm",
    )(x, g.reshape(1, D).astype(F32))


def _mm_kernel(a_ref, b_ref, o_ref):
    o_ref[...] = _dot(a_ref[...], b_ref[...]).astype(o_ref.dtype)


def _mm_res_kernel(a_ref, b_ref, r_ref, o_ref):
    o_ref[...] = (_dot(a_ref[...], b_ref[...]) + r_ref[...]).astype(o_ref.dtype)


def matmul(a, b, out_dtype, tn, residual=None, name="matmul"):
    M, K = a.shape
    N = b.shape[1]
    tm = min(M, 1024)
    tn = min(N, tn)
    in_specs = [pl.BlockSpec((tm, K), lambda i, j: (i, 0)), pl.BlockSpec((K, tn), lambda i, j: (0, j))]
    args = [a, b]
    body = _mm_kernel
    if residual is not None:
        in_specs.append(pl.BlockSpec((tm, tn), lambda i, j: (i, j)))
        args.append(residual)
        body = _mm_res_kernel
    return pl.pallas_call(
        body,
        grid=(M // tm, N // tn),
        in_specs=in_specs,
        out_specs=pl.BlockSpec((tm, tn), lambda i, j: (i, j)),
        out_shape=jax.ShapeDtypeStruct((M, N), out_dtype),
        compiler_params=_params("parallel", "parallel"),
        name=name,
    )(*args)


def _mlstm_kernel(q_ref, k_ref, v_ref, o_ref, g_ref, gb_ref, gain_ref, out_ref, cn_ref, m_ref,
                  *, batch, heads, dk, dv, chunk, nchunks):
    L = chunk

    @pl.when(pl.program_id(0) == 0)
    def _():
        cn_ref[...] = jnp.zeros_like(cn_ref)
        m_ref[...] = jnp.zeros_like(m_ref)

    tri_b = _tril(L, BF16)
    causal = _tril(L, jnp.int32) > 0
    ones_col = (lax.broadcasted_iota(jnp.int32, (L, LANE), 1) == 0).astype(F32)
    kscale = dk ** -0.5

    chains = [(b, h) for b in range(batch) for h in range(heads)]
    n_chain = len(chains)

    def per_chain(fn):
        return jnp.stack([fn(b, h) for b, h in chains], axis=0)

    def chunk_body(c, carry):
        r0 = pl.multiple_of(c * L, L)
        rows = pl.ds(r0, L)
        g = [g_ref[b, rows, :] + gb_ref[...] for b in range(batch)]
        bc = [_cumsum_rows(_log_sigmoid(g[b]), tri_b) for b in range(batch)]
        g_t = [x.T for x in g]
        bc_t = [x.T for x in bc]
        i_col = per_chain(lambda b, h: g[b][:, h:h + 1])
        i_row = per_chain(lambda b, h: g_t[b][h:h + 1, :])
        b_col = per_chain(lambda b, h: bc[b][:, heads + h:heads + h + 1])
        b_row = per_chain(lambda b, h: bc_t[b][heads + h:heads + h + 1, :])
        m_prev = m_ref[:, 0:1, 0:1]

        dmat = jnp.where(causal, b_col - b_row + i_row, NEG_INF)
        inter = b_col + m_prev
        m_t = jnp.maximum(jnp.max(dmat, axis=-1, keepdims=True), inter)
        a_inter = jnp.exp(inter - m_t)
        decay = jnp.exp(dmat - m_t)
        b_last = b_col[:, L - 1:L, :]
        dec = b_last - b_col + i_col
        m_new = jnp.maximum(b_last + m_prev, jnp.max(dec, axis=1, keepdims=True))
        wk = jnp.exp(dec - m_new)
        sc = jnp.exp(b_last + m_prev - m_new)

        qb = [q_ref[b, rows, h * dk:(h + 1) * dk].astype(BF16) for b, h in chains]
        kb = [(k_ref[b, rows, h * dk:(h + 1) * dk] * kscale).astype(BF16) for b, h in chains]
        v_aug = per_chain(lambda b, h: jnp.concatenate([v_ref[b, rows, h * dv:(h + 1) * dv], ones_col], axis=1))
        v_aug_b = v_aug.astype(BF16)
        s = (jnp.stack([_dot(qb[i], kb[i], NT_DIMS) for i in range(n_chain)], axis=0) * decay).astype(BF16)
        intra = jnp.stack([_dot(s[i], v_aug_b[i]) for i in range(n_chain)], axis=0)
        inter_out = jnp.stack([_dot(qb[i], cn_ref[i].astype(BF16), NT_DIMS) for i in range(n_chain)], axis=0)
        hn = intra + a_inter * inter_out

        wv = (wk * v_aug).astype(BF16)
        cn_ref[...] = sc * cn_ref[...] + jnp.stack([_dot(wv[i], kb[i], TN_DIMS) for i in range(n_chain)], axis=0)
        m_ref[...] = jnp.broadcast_to(m_new, m_ref.shape)

        hh = hn[:, :, :dv] / jnp.maximum(jnp.abs(hn[:, :, dv:dv + 1]), jnp.exp(-m_t))
        gain = per_chain(lambda b, h: gain_ref[:, h * dv:(h + 1) * dv])
        o_gate = _sigmoid(per_chain(lambda b, h: o_ref[b, rows, h * dv:(h + 1) * dv]))
        y = (hh * lax.rsqrt(jnp.mean(hh * hh, axis=-1, keepdims=True) + EPS) * gain * o_gate).astype(out_ref.dtype)
        for i, (b, h) in enumerate(chains):
            out_ref[b, rows, h * dv:(h + 1) * dv] = y[i]
        return carry

    lax.fori_loop(0, nchunks, chunk_body, 0)


def mlstm_block(z, gate_bias, head_gain, batch, seq, q_off, k_off, v_off, o_off, g_off):
    T, cols = z.shape
    H, L = MLSTM_HEADS, MLSTM_CHUNK
    width = head_gain.shape[0]
    dv = width // H
    dk = dv // 2
    ts = min(seq, 256)
    z3 = z.reshape(batch, seq, cols)

    def col(off, w):
        assert off % w == 0
        return pl.BlockSpec((batch, ts, w), lambda s: (0, s, off // w))

    n_state = batch * H
    out = pl.pallas_call(
        functools.partial(_mlstm_kernel, batch=batch, heads=H, dk=dk, dv=dv, chunk=L, nchunks=ts // L),
        grid=(seq // ts,),
        in_specs=[col(q_off, H * dk), col(k_off, H * dk), col(v_off, width), col(o_off, width), col(g_off, LANE),
                  pl.BlockSpec((1, LANE), lambda s: (0, 0)), pl.BlockSpec((1, width), lambda s: (0, 0))],
        out_specs=pl.BlockSpec((batch, ts, width), lambda s: (0, s, 0)),
        out_shape=jax.ShapeDtypeStruct((batch, seq, width), BF16),
        scratch_shapes=[pltpu.VMEM((n_state, dv + LANE, dk), F32),
                        pltpu.VMEM((n_state, SUBLANE, LANE), F32)],
        compiler_params=_params("arbitrary"),
        name="mlstm",
    )(z3, z3, z3, z3, z3, gate_bias, head_gain.reshape(1, width).astype(F32))
    return out.reshape(T, width)


def _pool_kernel(p_ref, w_ref, s_ref, out_ref, halo_ref, *, group, ts):
    si = pl.program_id(1)

    @pl.when(si == 0)
    def _():
        halo_ref[...] = jnp.zeros_like(halo_ref)

    p = p_ref[...]
    buf = jnp.concatenate([halo_ref[...], p], axis=0)
    t_abs = si * ts + lax.broadcasted_iota(jnp.int32, (ts, 1), 0)
    win = buf
    span = 1
    for g, w in enumerate(POOL_WINDOWS):
        while span < w:
            win = win + pltpu.roll(win, span, 0)
            span *= 2
        sl = slice(g * group, (g + 1) * group)
        cnt = jnp.minimum(t_abs + 1, w).astype(F32)
        mixed = win[POOL_HALO:, sl] / cnt - p[:, sl]
        y = _dot(mixed.astype(BF16), w_ref[g]) * s_ref[:, sl]
        out_ref[:, sl] = y.astype(out_ref.dtype)
    halo_ref[...] = p[ts - POOL_HALO:, :]


def pool_block(z, w_pool, pool_scale, batch, seq, p_off):
    T = z.shape[0]
    ng, group, _ = w_pool.shape
    width = ng * group
    ts = min(seq, 512)
    nblk = seq // ts
    assert p_off % width == 0
    return pl.pallas_call(
        functools.partial(_pool_kernel, group=group, ts=ts),
        grid=(batch, nblk),
        in_specs=[pl.BlockSpec((ts, width), lambda b, s: (b * nblk + s, p_off // width)),
                  pl.BlockSpec((ng, group, group), lambda b, s: (0, 0, 0)),
                  pl.BlockSpec((1, width), lambda b, s: (0, 0))],
        out_specs=pl.BlockSpec((ts, width), lambda b, s: (b * nblk + s, 0)),
        out_shape=jax.ShapeDtypeStruct((T, width), BF16),
        scratch_shapes=[pltpu.VMEM((POOL_HALO, width), F32)],
        compiler_params=_params("parallel", "arbitrary"),
        name="pool",
    )(z, w_pool.astype(BF16), pool_scale.reshape(1, width).astype(F32))


def _gla_kernel(q_ref, k_ref, v_ref, r_ref, gl_ref, wg_ref, bg_ref, gain_ref, out_ref, st_ref,
                *, batch, heads, dk, dv, chunk, nchunks):
    L = chunk
    Lh = L // 2

    @pl.when(pl.program_id(0) == 0)
    def _():
        st_ref[...] = jnp.zeros_like(st_ref)

    tri_b = _tril(L, BF16)
    row_id = lax.broadcasted_iota(jnp.int32, (L, 1), 0)
    row_half = lax.broadcasted_iota(jnp.int32, (Lh, 1), 0)
    lane_id = lax.broadcasted_iota(jnp.int32, (Lh, L), 1)
    qscale = dk ** -0.5
    chains = [(b, h) for b in range(batch) for h in range(heads)]
    n_chain = len(chains)

    def per_chain(fn):
        return jnp.stack([fn(b, h) for b, h in chains], axis=0)

    def chunk_body(c, carry):
        r0 = pl.multiple_of(c * L, L)
        rows = pl.ds(r0, L)
        b_seq = []
        for b in range(batch):
            la = _log_sigmoid(_dot(gl_ref[b, rows, :].astype(BF16), wg_ref[...]) + bg_ref[...]) / GLA_TAU
            b_seq.append(_cumsum_rows(la, tri_b))
        bb = per_chain(lambda b, h: b_seq[b][:, h * dk:(h + 1) * dk]) * LOG2E
        q = per_chain(lambda b, h: q_ref[b, rows, h * dk:(h + 1) * dk]) * qscale
        k = per_chain(lambda b, h: k_ref[b, rows, h * dk:(h + 1) * dk])

        att_half = [jnp.zeros((n_chain, Lh, L), F32), jnp.zeros((n_chain, Lh, L), F32)]
        for s in range(L):
            hf = s // Lh
            h0 = hf * Lh
            rel = jnp.where(row_half >= s - h0, bb[:, h0:h0 + Lh, :] - bb[:, s:s + 1, :], NEG_INF)
            col = jnp.sum(q[:, h0:h0 + Lh, :] * k[:, s:s + 1, :] * jnp.exp2(rel), axis=-1, keepdims=True)
            att_half[hf] = jnp.where(lane_id == s, col, att_half[hf])
        anchor = bb[:, Lh - 1:Lh, :]
        qd = (q * jnp.exp2(jnp.where(row_id >= Lh, bb - anchor, NEG_INF))).astype(BF16)
        kd = (k * jnp.exp2(jnp.where(row_id < Lh, anchor - bb, NEG_INF))).astype(BF16)
        att = jnp.concatenate(att_half, axis=1) + jnp.stack([_dot(qd[i], kd[i], NT_DIMS) for i in range(n_chain)], axis=0)
        att = att.astype(BF16)

        b_last = bb[:, L - 1:L, :]
        q_in = (q * jnp.exp2(bb)).astype(BF16)
        k_out = (k * jnp.exp2(b_last - bb)).astype(BF16)
        vb = [v_ref[b, rows, h * dv:(h + 1) * dv].astype(BF16) for b, h in chains]
        o = jnp.stack([_dot(att[i], vb[i]) + _dot(q_in[i], st_ref[i].astype(BF16), NT_DIMS) for i in range(n_chain)], axis=0)
        st_ref[...] = jnp.exp2(b_last) * st_ref[...] + jnp.stack([_dot(vb[i], k_out[i], TN_DIMS) for i in range(n_chain)], axis=0)

        gain = per_chain(lambda b, h: gain_ref[:, h * dv:(h + 1) * dv])
        r = per_chain(lambda b, h: r_ref[b, rows, h * dv:(h + 1) * dv])
        y = o * lax.rsqrt(jnp.mean(o * o, axis=-1, keepdims=True) + EPS) * gain * (r * _sigmoid(r))
        y = y.astype(out_ref.dtype)
        for i, (b, h) in enumerate(chains):
            out_ref[b, rows, h * dv:(h + 1) * dv] = y[i]
        return carry

    lax.fori_loop(0, nchunks, chunk_body, 0)


def gla_block(z, w_gate, b_gate, head_gain, batch, seq, q_off, k_off, v_off, r_off, g_off):
    T, cols = z.shape
    H, L = GLA_HEADS, GLA_CHUNK
    dkt = w_gate.shape[1]
    dvt = head_gain.shape[0]
    dk, dv = dkt // H, dvt // H
    ts = min(seq, 128)
    z3 = z.reshape(batch, seq, cols)

    def col(off, w):
        assert off % w == 0
        return pl.BlockSpec((batch, ts, w), lambda s: (0, s, off // w))

    out = pl.pallas_call(
        functools.partial(_gla_kernel, batch=batch, heads=H, dk=dk, dv=dv, chunk=L, nchunks=ts // L),
        grid=(seq // ts,),
        in_specs=[col(q_off, dkt), col(k_off, dkt), col(v_off, dvt), col(r_off, dvt), col(g_off, LANE),
                  pl.BlockSpec((LANE, dkt), lambda s: (0, 0)),
                  pl.BlockSpec((1, dkt), lambda s: (0, 0)),
                  pl.BlockSpec((1, dvt), lambda s: (0, 0))],
        out_specs=pl.BlockSpec((batch, ts, dvt), lambda s: (0, s, 0)),
        out_shape=jax.ShapeDtypeStruct((batch, seq, dvt), BF16),
        scratch_shapes=[pltpu.VMEM((batch * H, dv, dk), F32)],
        compiler_params=_params("arbitrary"),
        name="gla",
    )(z3, z3, z3, z3, z3, w_gate, b_gate.reshape(1, dkt).astype(F32), head_gain.reshape(1, dvt).astype(F32))
    return out.reshape(T, dvt)


def _extract_topk(vals, k):
    rank = jnp.full(vals.shape, float(k), F32)
    tops = []
    for i in range(k):
        mx = jnp.max(vals, axis=0, keepdims=True)
        hit = vals == mx
        rank = jnp.where(hit, float(i), rank)
        vals = jnp.where(hit, NEG_INF, vals)
        tops.append(mx)
    return jnp.concatenate(tops, axis=0), rank


def _dup_bf16(x):
    hi = pltpu.bitcast(x.astype(BF16).astype(F32), jnp.uint32)
    return hi | lax.shift_right_logical(hi, jnp.uint32(16))


def _peer_topk_kernel(qt_ref, keys_ref, cnt1_ref, p1_ref, r2_ref, p2_ref, *, heads, half, topk):
    K = topk
    for h in range(heads):
        s1 = _dot(keys_ref[h, 0], qt_ref[(2 * h) * half:(2 * h + 1) * half, :])
        s2 = _dot(keys_ref[h, 1], qt_ref[(2 * h + 1) * half:(2 * h + 2) * half, :])
        sv1, rank1 = _extract_topk(s1, K)
        sv2, rank2 = _extract_topk(s2, K)
        cand = []
        a = 0
        while K // (a + 1) > 1:
            nb = K // (a + 1)
            rows = -(-nb // SUBLANE) * SUBLANE
            c = sv1[a:a + 1, :] + sv2[:rows, :]
            if nb < rows:
                c = jnp.where(lax.broadcasted_iota(jnp.int32, (rows, 1), 0) < nb, c, NEG_INF)
            cand.append(c)
            a += 1
        n_single = a
        cand.append(sv1[n_single:, :] + sv2[0:1, :])
        work = jnp.concatenate(cand, axis=0)
        tau = None
        for _ in range(K):
            tau = jnp.max(work, axis=0, keepdims=True)
            work = jnp.where(work == tau, NEG_INF, work)
        cmax = sv1[0:1, :] + sv2[0:1, :]
        zsum = jnp.zeros_like(cmax)
        cnt1 = jnp.zeros_like(s1)
        for a, c in enumerate(cand):
            sel = c >= tau
            zsum = zsum + jnp.sum(jnp.where(sel, jnp.exp(c - cmax), 0.0), axis=0, keepdims=True)
            if a < n_single:
                cnt1 = jnp.where(rank1 == float(a), jnp.sum(sel.astype(F32), axis=0, keepdims=True), cnt1)
            else:
                for j in range(K - n_single):
                    cnt1 = jnp.where(rank1 == float(a + j), sel[j:j + 1, :].astype(F32), cnt1)
        cnt1_ref[h] = _dup_bf16(cnt1)
        p1_ref[h] = _dup_bf16(jnp.exp(s1 - sv1[0:1, :]) / zsum)
        r2_ref[h] = pltpu.bitcast(rank2.astype(BF16), jnp.uint32)
        p2_ref[h] = pltpu.bitcast(jnp.exp(s2 - sv2[0:1, :]).astype(BF16), jnp.uint32)


def peer_topk(qt, keys):
    heads, _, n_keys, half = keys.shape
    T = qt.shape[1]
    tt = min(T, 256)
    out_f = jax.ShapeDtypeStruct((heads, n_keys, T), jnp.uint32)
    out_b = jax.ShapeDtypeStruct((heads, n_keys // 2, T), jnp.uint32)
    ospec = pl.BlockSpec((heads, n_keys, tt), lambda i: (0, 0, i))
    ospec_b = pl.BlockSpec((heads, n_keys // 2, tt), lambda i: (0, 0, i))
    return pl.pallas_call(
        functools.partial(_peer_topk_kernel, heads=heads, half=half, topk=PEER_TOPK),
        grid=(T // tt,),
        in_specs=[pl.BlockSpec((heads * 2 * half, tt), lambda i: (0, i)),
                  pl.BlockSpec((heads, 2, n_keys, half), lambda i: (0, 0, 0, 0))],
        out_specs=[ospec, ospec, ospec_b, ospec_b],
        out_shape=[out_f, out_f, out_b, out_b],
        compiler_params=_params("parallel"),
        name="peer_topk",
    )(qt, keys)


def _peer_gate_tile(a, cnt_rows, p1_rows, r2_ref, p2_ref, lanes, *, heads, n_keys):
    ab = a.astype(BF16)
    gelu = 0.5 * ab * (1.0 + lax.erf(ab * (2.0 ** -0.5)))
    gate = jnp.zeros((n_keys, LANE), BF16)
    for h in range(heads):
        cnt = pltpu.bitcast(jnp.broadcast_to(cnt_rows[h][:, lanes], (n_keys // 2, LANE)), BF16)
        p1 = pltpu.bitcast(jnp.broadcast_to(p1_rows[h][:, lanes], (n_keys // 2, LANE)), BF16)
        r2 = pltpu.bitcast(r2_ref[h, :, lanes], BF16)
        p2 = pltpu.bitcast(p2_ref[h, :, lanes], BF16)
        gate = gate + jnp.where(r2 < cnt, p2 * p1, jnp.zeros_like(gate))
    return (gate * gelu).T


def _peer_dense_step(blk, c_new, c_old, xt_ref, u_ref, v_ref, cnt1_ref, p1_ref, r2_ref, p2_ref, out_ref,
                     *, heads, n_keys, groups):
    tm = xt_ref.shape[1]
    dc = out_ref.shape[1] // groups
    for r in range(groups):
        e1 = blk * groups + r
        rows = slice(r * n_keys, (r + 1) * n_keys)
        a = _dot(u_ref[rows, :], xt_ref[...])
        cnt_rows = [cnt1_ref[h, pl.ds(e1, 1), :] for h in range(heads)]
        p1_rows = [p1_ref[h, pl.ds(e1, 1), :] for h in range(heads)]
        for lt in range(tm // LANE):
            lanes = slice(lt * LANE, (lt + 1) * LANE)
            c_new[lanes, rows] = _peer_gate_tile(a[:, lanes], cnt_rows, p1_rows, r2_ref, p2_ref, lanes,
                                                 heads=heads, n_keys=n_keys)
        cols = slice(r * dc, (r + 1) * dc)
        out_ref[:, cols] += _dot(c_old[...], v_ref[:, cols])


def _peer_dense_kernel(xt_ref, u_ref, v_ref, cnt1_ref, p1_ref, r2_ref, p2_ref, res_ref, gain_ref, *rest,
                       heads, n_keys, groups, nblk, final):
    if final:
        out_ref, c0_ref, c1_ref = rest
    else:
        out_ref, norm_ref, c0_ref, c1_ref = rest
    g = pl.program_id(1)
    step = functools.partial(_peer_dense_step, jnp.minimum(g, nblk - 1), xt_ref=xt_ref, u_ref=u_ref, v_ref=v_ref,
                             cnt1_ref=cnt1_ref, p1_ref=p1_ref, r2_ref=r2_ref, p2_ref=p2_ref, out_ref=out_ref,
                             heads=heads, n_keys=n_keys, groups=groups)

    @pl.when(g == 0)
    def _():
        out_ref[...] = res_ref[...]
        c1_ref[...] = jnp.zeros_like(c1_ref)

    @pl.when(lax.rem(g, 2) == 0)
    def _():
        step(c0_ref, c1_ref)

    @pl.when(lax.rem(g, 2) == 1)
    def _():
        step(c1_ref, c0_ref)

    @pl.when(g == nblk)
    def _():
        x = out_ref[...]
        y = x * lax.rsqrt(jnp.mean(x * x, axis=-1, keepdims=True) + EPS) * gain_ref[...]
        if final:
            out_ref[...] = y
        else:
            norm_ref[...] = y.astype(norm_ref.dtype)


def peer_dense(xt, u, v, cnt1, p1, r2, p2, residual, post_gain, final):
    D, T = xt.shape
    heads, n_keys, _ = cnt1.shape
    E = u.shape[0]
    groups = 4
    eb = groups * n_keys
    nblk = E // eb
    tm = min(T, 512)
    gspec = pl.BlockSpec((heads, n_keys, tm), lambda i, g: (0, 0, i))
    gspec_b = pl.BlockSpec((heads, n_keys // 2, tm), lambda i, g: (0, 0, i))
    row_spec = pl.BlockSpec((tm, D), lambda i, g: (i, 0))
    out_specs = row_spec if final else [row_spec, row_spec]
    out_shape = jax.ShapeDtypeStruct((T, D), F32)
    if not final:
        out_shape = [out_shape, jax.ShapeDtypeStruct((T, D), BF16)]
    return pl.pallas_call(
        functools.partial(_peer_dense_kernel, heads=heads, n_keys=n_keys, groups=groups, nblk=nblk, final=final),
        grid=(T // tm, nblk + 1),
        in_specs=[pl.BlockSpec((D, tm), lambda i, g: (0, i)),
                  pl.BlockSpec((eb, D), lambda i, g: (jnp.minimum(g, nblk - 1), 0)),
                  pl.BlockSpec((eb, D), lambda i, g: (jnp.maximum(g - 1, 0), 0)),
                  gspec, gspec, gspec_b, gspec_b,
                  pl.BlockSpec((tm, D), lambda i, g: (i, 0), pipeline_mode=pl.Buffered(1)),
                  pl.BlockSpec((1, D), lambda i, g: (0, 0))],
        out_specs=out_specs,
        out_shape=out_shape,
        scratch_shapes=[pltpu.VMEM((tm, eb), BF16), pltpu.VMEM((tm, eb), BF16)],
        compiler_params=_params("parallel", "arbitrary"),
        name="peer_dense",
    )(xt, u, v, cnt1, p1, r2, p2, residual, post_gain.reshape(1, D).astype(F32))


def _cast_kernel(x_ref, o_ref):
    o_ref[...] = x_ref[0].astype(o_ref.dtype)


def cast_table(tabs, layer, dtype):
    _, rows, cols = tabs.shape
    tr = min(rows, 1024)
    return pl.pallas_call(
        _cast_kernel,
        grid=(rows // tr,),
        in_specs=[pl.BlockSpec((1, tr, cols), lambda i: (layer, i, 0))],
        out_specs=pl.BlockSpec((tr, cols), lambda i: (i, 0)),
        out_shape=jax.ShapeDtypeStruct((rows, cols), dtype),
        compiler_params=_params("parallel"),
        name="cast_table",
    )(tabs)


def peer_layer(x, norm_g, w_q, sub_keys, u_tabs, v_tabs, layer, post_gain, final):
    ht = rmsnorm(x, norm_g, BF16, transpose=True)
    qt = matmul(w_q.T.astype(BF16), ht, BF16, tn=512, name="peer_q")
    cnt1, p1, r2, p2 = peer_topk(qt, sub_keys.astype(BF16))
    return peer_dense(ht, cast_table(u_tabs, layer, BF16), cast_table(v_tabs, layer, BF16), cnt1, p1, r2, p2, x,
                      post_gain, final)


def _pad_cols(w, n):
    return jnp.pad(w, ((0, 0), (0, n - w.shape[1])))


def even_layer(x, h, batch, seq, w_in, b_igate, b_fgate, head_gain, w_pool, pool_scale, w_out):
    H = MLSTM_HEADS
    width = head_gain.shape[0]
    qk = width // 2
    pool_w = pool_scale.shape[0]
    n_main = 2 * qk + 2 * width
    w_cat = jnp.concatenate([w_in[:, :n_main], w_in[:, n_main + 2 * H:], w_in[:, n_main:n_main + 2 * H]], axis=1)
    tn = 3 * MXU_COLS
    z = matmul(h, _pad_cols(w_cat, -(-w_cat.shape[1] // tn) * tn).astype(BF16), F32, tn=tn, name="even_in")
    gate_bias = jnp.pad(jnp.concatenate([b_igate, b_fgate]), (0, LANE - 2 * H)).reshape(1, LANE).astype(F32)
    hm = mlstm_block(z, gate_bias, head_gain, batch, seq, 0, qk, 2 * qk, 2 * qk + width, n_main + pool_w)
    hp = pool_block(z, w_pool, pool_scale, batch, seq, n_main)
    y = jnp.concatenate([hm, hp], axis=1)
    return matmul(y, w_out.astype(BF16), F32, tn=4 * MXU_COLS, residual=x, name="even_out")


def odd_layer(x, h, batch, seq, w_in, w_gate, b_gate, head_gain, w_out):
    rank, dkt = w_gate.shape
    dvt = head_gain.shape[0]
    n_main = 2 * dkt + 2 * dvt
    tn = 5 * MXU_COLS
    z = matmul(h, _pad_cols(w_in, -(-w_in.shape[1] // tn) * tn).astype(BF16), F32, tn=tn, name="odd_in")
    wg = jnp.pad(w_gate, ((0, LANE - rank), (0, 0))).astype(BF16)
    y = gla_block(z, wg, b_gate, head_gain, batch, seq, 0, dkt, 2 * dkt, 2 * dkt + dvt, n_main)
    return matmul(y, w_out.astype(BF16), F32, tn=4 * MXU_COLS, residual=x, name="odd_out")


def kernel(x, e_norm, e_w_in, e_b_igate, e_b_fgate, e_head_gain, e_w_pool, e_pool_scale, e_w_out, o_norm, o_w_in, o_w_gate, o_b_gate, o_head_gain, o_w_out, f_norm, f_w_q, f_sub_keys, f_u, f_v, final_norm):
    B, S, D = x.shape
    depth = f_norm.shape[0]

    def mixer_gain(layer):
        return e_norm[layer // 2] if layer % 2 == 0 else o_norm[layer // 2]

    xs = x.reshape(B * S, D)
    h = rmsnorm(xs, mixer_gain(0), BF16)
    for layer in range(depth):
        j = layer // 2
        if layer % 2 == 0:
            xs = even_layer(xs, h, B, S, e_w_in[j], e_b_igate[j], e_b_fgate[j], e_head_gain[j],
                            e_w_pool[j], e_pool_scale[j], e_w_out[j])
        else:
            xs = odd_layer(xs, h, B, S, o_w_in[j], o_w_gate[j], o_b_gate[j], o_head_gain[j], o_w_out[j])
        if layer == depth - 1:
            return peer_layer(xs, f_norm[layer], f_w_q[layer], f_sub_keys[layer], f_u, f_v, layer,
                              final_norm, True).reshape(B, S, D)
        xs, h = peer_layer(xs, f_norm[layer], f_w_q[layer], f_sub_keys[layer], f_u, f_v, layer,
                           mixer_gain(layer + 1), False)
```

```python
import functools

import jax
import jax.numpy as jnp
from jax import lax
from jax.experimental import pallas as pl
from jax.experimental.pallas import tpu as pltpu

F32 = jnp.float32
BF16 = jnp.bfloat16
EPS = 1e-6
LANE = 128
SUBLANE = 8
MXU_COLS = 256
VMEM_LIMIT = 56 * 1024 * 1024

MLSTM_HEADS = 4
MLSTM_CHUNK = 64
POOL_WINDOWS = (2, 4, 8, 16)
POOL_HALO = 16
GLA_HEADS = 4
GLA_CHUNK = 32
GLA_TAU = 16.0
PEER_TOPK = 16
NEG_INF = float("-inf")
LOG2E = 1.4426950408889634

NT_DIMS = (((1,), (1,)), ((), ()))
TN_DIMS = (((0,), (0,)), ((), ()))


def _params(*sem):
    return pltpu.CompilerParams(dimension_semantics=sem, vmem_limit_bytes=VMEM_LIMIT)


def _log_sigmoid(x):
    return jnp.minimum(x, 0.0) - jnp.log1p(jnp.exp(-jnp.abs(x)))


def _sigmoid(x):
    return 1.0 / (1.0 + jnp.exp(-x))


def _dot(a, b, dims=None):
    if dims is None:
        return jnp.dot(a, b, preferred_element_type=F32)
    return lax.dot_general(a, b, dims, preferred_element_type=F32)


def _cumsum_rows(x, tri):
    hi = x.astype(BF16)
    lo = (x - hi.astype(F32)).astype(BF16)
    return _dot(tri, hi) + _dot(tri, lo)


def _tril(n, dtype):
    r = lax.broadcasted_iota(jnp.int32, (n, n), 0)
    c = lax.broadcasted_iota(jnp.int32, (n, n), 1)
    return (c <= r).astype(dtype)


def _rms_kernel(x_ref, g_ref, o_ref):
    x = x_ref[...]
    y = x * lax.rsqrt(jnp.mean(x * x, axis=-1, keepdims=True) + EPS) * g_ref[...]
    o_ref[...] = y.astype(o_ref.dtype)


def rmsnorm(x, g, out_dtype):
    T, D = x.shape
    tm = min(T, 512)
    return pl.pallas_call(
        _rms_kernel,
        grid=(T // tm,),
        in_specs=[pl.BlockSpec((tm, D), lambda i: (i, 0)), pl.BlockSpec((1, D), lambda i: (0, 0))],
        out_specs=pl.BlockSpec((tm, D), lambda i: (i, 0)),
        out_shape=jax.ShapeDtypeStruct((T, D), out_dtype),
        compiler_params=_params("parallel"),
        name="rmsnorm",
    )(x, g.reshape(1, D).astype(F32))


def _mm_kernel(a_ref, b_ref, o_ref):
    o_ref[...] = _dot(a_ref[...], b_ref[...]).astype(o_ref.dtype)


def matmul(a, b, out_dtype, tn, name="matmul"):
    M, K = a.shape
    N = b.shape[1]
    tm = min(M, 1024)
    tn = min(N, tn)
    return pl.pallas_call(
        _mm_kernel,
        grid=(M // tm, N // tn),
        in_specs=[pl.BlockSpec((tm, K), lambda i, j: (i, 0)), pl.BlockSpec((K, tn), lambda i, j: (0, j))],
        out_specs=pl.BlockSpec((tm, tn), lambda i, j: (i, j)),
        out_shape=jax.ShapeDtypeStruct((M, N), out_dtype),
        compiler_params=_params("parallel", "parallel"),
        name=name,
    )(a, b)


def _out_proj_kernel(*refs, n_parts):
    parts = refs[:n_parts]
    w_ref, res_ref, gain_ref, x_ref, nt_ref = refs[n_parts:]
    acc = res_ref[...]
    k0 = 0
    for p in parts:
        kp = p.shape[1]
        acc = acc + _dot(p[...], w_ref[k0:k0 + kp, :])
        k0 += kp
    x_ref[...] = acc
    y = acc * lax.rsqrt(jnp.mean(acc * acc, axis=-1, keepdims=True) + EPS) * gain_ref[...]
    nt_ref[...] = y.T.astype(nt_ref.dtype)


def out_proj(parts, w, residual, norm_gain):
    T, N = residual.shape
    K = w.shape[0]
    tm = min(T, 512)
    in_specs = [pl.BlockSpec((tm, p.shape[1]), lambda i: (i, 0)) for p in parts]
    in_specs += [pl.BlockSpec((K, N), lambda i: (0, 0), pipeline_mode=pl.Buffered(1)),
                 pl.BlockSpec((tm, N), lambda i: (i, 0)),
                 pl.BlockSpec((1, N), lambda i: (0, 0))]
    return pl.pallas_call(
        functools.partial(_out_proj_kernel, n_parts=len(parts)),
        grid=(T // tm,),
        in_specs=in_specs,
        out_specs=[pl.BlockSpec((tm, N), lambda i: (i, 0)), pl.BlockSpec((N, tm), lambda i: (0, i))],
        out_shape=[jax.ShapeDtypeStruct((T, N), F32), jax.ShapeDtypeStruct((N, T), BF16)],
        compiler_params=_params("parallel"),
        name="out_proj",
    )(*parts, w, residual, norm_gain.reshape(1, N).astype(F32))


def _mlstm_kernel(q_ref, k_ref, v_ref, o_ref, g_ref, gb_ref, gain_ref, out_ref, cn_ref, m_ref,
                  *, batch, heads, dk, dv, chunk, nchunks):
    L = chunk

    @pl.when(pl.program_id(0) == 0)
    def _():
        cn_ref[...] = jnp.zeros_like(cn_ref)
        m_ref[...] = jnp.zeros_like(m_ref)

    tri_b = _tril(L, BF16)
    causal = _tril(L, jnp.int32) > 0
    ones_col = (lax.broadcasted_iota(jnp.int32, (L, LANE), 1) == 0).astype(F32)
    kscale = dk ** -0.5

    chains = [(b, h) for b in range(batch) for h in range(heads)]
    n_chain = len(chains)

    def per_chain(fn):
        return jnp.stack([fn(b, h) for b, h in chains], axis=0)

    def chunk_body(c, carry):
        r0 = pl.multiple_of(c * L, L)
        rows = pl.ds(r0, L)
        g = [g_ref[b, rows, :] + gb_ref[...] for b in range(batch)]
        bc = [_cumsum_rows(_log_sigmoid(g[b]), tri_b) for b in range(batch)]
        g_t = [x.T for x in g]
        bc_t = [x.T for x in bc]
        i_col = per_chain(lambda b, h: g[b][:, h:h + 1])
        i_row = per_chain(lambda b, h: g_t[b][h:h + 1, :])
        b_col = per_chain(lambda b, h: bc[b][:, heads + h:heads + h + 1])
        b_row = per_chain(lambda b, h: bc_t[b][heads + h:heads + h + 1, :])
        m_prev = m_ref[:, 0:1, 0:1]

        dmat = jnp.where(causal, b_col - b_row + i_row, NEG_INF)
        inter = b_col + m_prev
        m_t = jnp.maximum(jnp.max(dmat, axis=-1, keepdims=True), inter)
        a_inter = jnp.exp(inter - m_t)
        decay = jnp.exp(dmat - m_t)
        b_last = b_col[:, L - 1:L, :]
        dec = b_last - b_col + i_col
        m_new = jnp.maximum(b_last + m_prev, jnp.max(dec, axis=1, keepdims=True))
        wk = jnp.exp(dec - m_new)
        sc = jnp.exp(b_last + m_prev - m_new)

        qb = [q_ref[b, rows, h * dk:(h + 1) * dk].astype(BF16) for b, h in chains]
        kb = [(k_ref[b, rows, h * dk:(h + 1) * dk] * kscale).astype(BF16) for b, h in chains]
        v_aug = per_chain(lambda b, h: jnp.concatenate([v_ref[b, rows, h * dv:(h + 1) * dv], ones_col], axis=1))
        v_aug_b = v_aug.astype(BF16)
        s = (jnp.stack([_dot(qb[i], kb[i], NT_DIMS) for i in range(n_chain)], axis=0) * decay).astype(BF16)
        intra = jnp.stack([_dot(s[i], v_aug_b[i]) for i in range(n_chain)], axis=0)
        inter_out = jnp.stack([_dot(qb[i], cn_ref[i].astype(BF16), NT_DIMS) for i in range(n_chain)], axis=0)
        hn = intra + a_inter * inter_out

        wv = (wk * v_aug).astype(BF16)
        cn_ref[...] = sc * cn_ref[...] + jnp.stack([_dot(wv[i], kb[i], TN_DIMS) for i in range(n_chain)], axis=0)
        m_ref[...] = jnp.broadcast_to(m_new, m_ref.shape)

        hh = hn[:, :, :dv] / jnp.maximum(jnp.abs(hn[:, :, dv:dv + 1]), jnp.exp(-m_t))
        gain = per_chain(lambda b, h: gain_ref[:, h * dv:(h + 1) * dv])
        o_gate = _sigmoid(per_chain(lambda b, h: o_ref[b, rows, h * dv:(h + 1) * dv]))
        y = (hh * lax.rsqrt(jnp.mean(hh * hh, axis=-1, keepdims=True) + EPS) * gain * o_gate).astype(out_ref.dtype)
        for i, (b, h) in enumerate(chains):
            out_ref[b, rows, h * dv:(h + 1) * dv] = y[i]
        return carry

    lax.fori_loop(0, nchunks, chunk_body, 0)


def mlstm_block(z, gate_bias, head_gain, batch, seq, q_off, k_off, v_off, o_off, g_off):
    T, cols = z.shape
    H, L = MLSTM_HEADS, MLSTM_CHUNK
    width = head_gain.shape[0]
    dv = width // H
    dk = dv // 2
    ts = min(seq, 256)
    z3 = z.reshape(batch, seq, cols)

    def col(off, w):
        assert off % w == 0
        return pl.BlockSpec((batch, ts, w), lambda s: (0, s, off // w))

    n_state = batch * H
    out = pl.pallas_call(
        functools.partial(_mlstm_kernel, batch=batch, heads=H, dk=dk, dv=dv, chunk=L, nchunks=ts // L),
        grid=(seq // ts,),
        in_specs=[col(q_off, H * dk), col(k_off, H * dk), col(v_off, width), col(o_off, width), col(g_off, LANE),
                  pl.BlockSpec((1, LANE), lambda s: (0, 0)), pl.BlockSpec((1, width), lambda s: (0, 0))],
        out_specs=pl.BlockSpec((batch, ts, width), lambda s: (0, s, 0)),
        out_shape=jax.ShapeDtypeStruct((batch, seq, width), BF16),
        scratch_shapes=[pltpu.VMEM((n_state, dv + LANE, dk), F32),
                        pltpu.VMEM((n_state, SUBLANE, LANE), F32)],
        compiler_params=_params("arbitrary"),
        name="mlstm",
    )(z3, z3, z3, z3, z3, gate_bias, head_gain.reshape(1, width).astype(F32))
    return out.reshape(T, width)


def _pool_kernel(p_ref, w_ref, s_ref, out_ref, halo_ref, *, group, ts):
    si = pl.program_id(1)

    @pl.when(si == 0)
    def _():
        halo_ref[...] = jnp.zeros_like(halo_ref)

    p = p_ref[...]
    buf = jnp.concatenate([halo_ref[...], p], axis=0)
    t_abs = si * ts + lax.broadcasted_iota(jnp.int32, (ts, 1), 0)
    win = buf
    span = 1
    for g, w in enumerate(POOL_WINDOWS):
        while span < w:
            win = win + pltpu.roll(win, span, 0)
            span *= 2
        sl = slice(g * group, (g + 1) * group)
        cnt = jnp.minimum(t_abs + 1, w).astype(F32)
        mixed = win[POOL_HALO:, sl] / cnt - p[:, sl]
        y = _dot(mixed.astype(BF16), w_ref[g]) * s_ref[:, sl]
        out_ref[:, sl] = y.astype(out_ref.dtype)
    halo_ref[...] = p[ts - POOL_HALO:, :]


def pool_block(z, w_pool, pool_scale, batch, seq, p_off):
    T = z.shape[0]
    ng, group, _ = w_pool.shape
    width = ng * group
    ts = min(seq, 512)
    nblk = seq // ts
    assert p_off % width == 0
    return pl.pallas_call(
        functools.partial(_pool_kernel, group=group, ts=ts),
        grid=(batch, nblk),
        in_specs=[pl.BlockSpec((ts, width), lambda b, s: (b * nblk + s, p_off // width)),
                  pl.BlockSpec((ng, group, group), lambda b, s: (0, 0, 0)),
                  pl.BlockSpec((1, width), lambda b, s: (0, 0))],
        out_specs=pl.BlockSpec((ts, width), lambda b, s: (b * nblk + s, 0)),
        out_shape=jax.ShapeDtypeStruct((T, width), BF16),
        scratch_shapes=[pltpu.VMEM((POOL_HALO, width), F32)],
        compiler_params=_params("parallel", "arbitrary"),
        name="pool",
    )(z, w_pool.astype(BF16), pool_scale.reshape(1, width).astype(F32))


def _gla_kernel(q_ref, k_ref, v_ref, r_ref, gl_ref, wg_ref, bg_ref, gain_ref, out_ref, st_ref,
                *, batch, heads, dk, dv, chunk, nchunks):
    L = chunk
    Lh = L // 2

    @pl.when(pl.program_id(0) == 0)
    def _():
        st_ref[...] = jnp.zeros_like(st_ref)

    tri_b = _tril(L, BF16)
    row_id = lax.broadcasted_iota(jnp.int32, (L, 1), 0)
    row_half = lax.broadcasted_iota(jnp.int32, (Lh, 1), 0)
    lane_id = lax.broadcasted_iota(jnp.int32, (Lh, L), 1)
    qscale = dk ** -0.5
    chains = [(b, h) for b in range(batch) for h in range(heads)]
    n_chain = len(chains)

    def per_chain(fn):
        return jnp.stack([fn(b, h) for b, h in chains], axis=0)

    def chunk_body(c, carry):
        r0 = pl.multiple_of(c * L, L)
        rows = pl.ds(r0, L)
        b_seq = []
        for b in range(batch):
            la = _log_sigmoid(_dot(gl_ref[b, rows, :].astype(BF16), wg_ref[...]) + bg_ref[...]) / GLA_TAU
            b_seq.append(_cumsum_rows(la, tri_b))
        bb = per_chain(lambda b, h: b_seq[b][:, h * dk:(h + 1) * dk]) * LOG2E
        q = per_chain(lambda b, h: q_ref[b, rows, h * dk:(h + 1) * dk]) * qscale
        k = per_chain(lambda b, h: k_ref[b, rows, h * dk:(h + 1) * dk])

        att_half = [jnp.zeros((n_chain, Lh, L), F32), jnp.zeros((n_chain, Lh, L), F32)]
        for s in range(L):
            hf = s // Lh
            h0 = hf * Lh
            rel = jnp.where(row_half >= s - h0, bb[:, h0:h0 + Lh, :] - bb[:, s:s + 1, :], NEG_INF)
            col = jnp.sum(q[:, h0:h0 + Lh, :] * k[:, s:s + 1, :] * jnp.exp2(rel), axis=-1, keepdims=True)
            att_half[hf] = jnp.where(lane_id == s, col, att_half[hf])
        anchor = bb[:, Lh - 1:Lh, :]
        qd = (q * jnp.exp2(jnp.where(row_id >= Lh, bb - anchor, NEG_INF))).astype(BF16)
        kd = (k * jnp.exp2(jnp.where(row_id < Lh, anchor - bb, NEG_INF))).astype(BF16)
        att = jnp.concatenate(att_half, axis=1) + jnp.stack([_dot(qd[i], kd[i], NT_DIMS) for i in range(n_chain)], axis=0)
        att = att.astype(BF16)

        b_last = bb[:, L - 1:L, :]
        q_in = (q * jnp.exp2(bb)).astype(BF16)
        k_out = (k * jnp.exp2(b_last - bb)).astype(BF16)
        vb = [v_ref[b, rows, h * dv:(h + 1) * dv].astype(BF16) for b, h in chains]
        o = jnp.stack([_dot(att[i], vb[i]) + _dot(q_in[i], st_ref[i].astype(BF16), NT_DIMS) for i in range(n_chain)], axis=0)
        st_ref[...] = jnp.exp2(b_last) * st_ref[...] + jnp.stack([_dot(vb[i], k_out[i], TN_DIMS) for i in range(n_chain)], axis=0)

        gain = per_chain(lambda b, h: gain_ref[:, h * dv:(h + 1) * dv])
        r = per_chain(lambda b, h: r_ref[b, rows, h * dv:(h + 1) * dv])
        y = o * lax.rsqrt(jnp.mean(o * o, axis=-1, keepdims=True) + EPS) * gain * (r * _sigmoid(r))
        y = y.astype(out_ref.dtype)
        for i, (b, h) in enumerate(chains):
            out_ref[b, rows, h * dv:(h + 1) * dv] = y[i]
        return carry

    lax.fori_loop(0, nchunks, chunk_body, 0)


def gla_block(z, w_gate, b_gate, head_gain, batch, seq, q_off, k_off, v_off, r_off, g_off):
    T, cols = z.shape
    H, L = GLA_HEADS, GLA_CHUNK
    dkt = w_gate.shape[1]
    dvt = head_gain.shape[0]
    dk, dv = dkt // H, dvt // H
    ts = min(seq, 128)
    z3 = z.reshape(batch, seq, cols)

    def col(off, w):
        assert off % w == 0
        return pl.BlockSpec((batch, ts, w), lambda s: (0, s, off // w))

    out = pl.pallas_call(
        functools.partial(_gla_kernel, batch=batch, heads=H, dk=dk, dv=dv, chunk=L, nchunks=ts // L),
        grid=(seq // ts,),
        in_specs=[col(q_off, dkt), col(k_off, dkt), col(v_off, dvt), col(r_off, dvt), col(g_off, LANE),
                  pl.BlockSpec((LANE, dkt), lambda s: (0, 0)),
                  pl.BlockSpec((1, dkt), lambda s: (0, 0)),
                  pl.BlockSpec((1, dvt), lambda s: (0, 0))],
        out_specs=pl.BlockSpec((batch, ts, dvt), lambda s: (0, s, 0)),
        out_shape=jax.ShapeDtypeStruct((batch, seq, dvt), BF16),
        scratch_shapes=[pltpu.VMEM((batch * H, dv, dk), F32)],
        compiler_params=_params("arbitrary"),
        name="gla",
    )(z3, z3, z3, z3, z3, w_gate, b_gate.reshape(1, dkt).astype(F32), head_gain.reshape(1, dvt).astype(F32))
    return out.reshape(T, dvt)


def _extract_topk(vals, k):
    rank = jnp.full(vals.shape, float(k), F32)
    tops = []
    for i in range(k):
        mx = jnp.max(vals, axis=0, keepdims=True)
        hit = vals == mx
        rank = jnp.where(hit, float(i), rank)
        vals = jnp.where(hit, NEG_INF, vals)
        tops.append(mx)
    return jnp.concatenate(tops, axis=0), rank


def _dup_bf16(x):
    hi = pltpu.bitcast(x.astype(BF16).astype(F32), jnp.uint32)
    return hi | lax.shift_right_logical(hi, jnp.uint32(16))


def _peer_topk_kernel(qt_ref, keys_ref, cnt1_ref, p1_ref, r2_ref, p2_ref, *, heads, half, topk):
    K = topk
    for h in range(heads):
        s1 = _dot(keys_ref[h, 0], qt_ref[(2 * h) * half:(2 * h + 1) * half, :])
        s2 = _dot(keys_ref[h, 1], qt_ref[(2 * h + 1) * half:(2 * h + 2) * half, :])
        sv1, rank1 = _extract_topk(s1, K)
        sv2, rank2 = _extract_topk(s2, K)
        cand = []
        a = 0
        while K // (a + 1) > 1:
            nb = K // (a + 1)
            rows = -(-nb // SUBLANE) * SUBLANE
            c = sv1[a:a + 1, :] + sv2[:rows, :]
            if nb < rows:
                c = jnp.where(lax.broadcasted_iota(jnp.int32, (rows, 1), 0) < nb, c, NEG_INF)
            cand.append(c)
            a += 1
        n_single = a
        cand.append(sv1[n_single:, :] + sv2[0:1, :])
        work = jnp.concatenate(cand, axis=0)
        tau = None
        for _ in range(K):
            tau = jnp.max(work, axis=0, keepdims=True)
            work = jnp.where(work == tau, NEG_INF, work)
        cmax = sv1[0:1, :] + sv2[0:1, :]
        zsum = jnp.zeros_like(cmax)
        cnt1 = jnp.zeros_like(s1)
        for a, c in enumerate(cand):
            sel = c >= tau
            zsum = zsum + jnp.sum(jnp.where(sel, jnp.exp(c - cmax), 0.0), axis=0, keepdims=True)
            if a < n_single:
                cnt1 = jnp.where(rank1 == float(a), jnp.sum(sel.astype(F32), axis=0, keepdims=True), cnt1)
            else:
                for j in range(K - n_single):
                    cnt1 = jnp.where(rank1 == float(a + j), sel[j:j + 1, :].astype(F32), cnt1)
        cnt1_ref[h] = _dup_bf16(cnt1)
        p1_ref[h] = _dup_bf16(jnp.exp(s1 - sv1[0:1, :]) / zsum)
        r2_ref[h] = pltpu.bitcast(rank2.astype(BF16), jnp.uint32)
        p2_ref[h] = pltpu.bitcast(jnp.exp(s2 - sv2[0:1, :]).astype(BF16), jnp.uint32)


def peer_topk(qt, keys):
    heads, _, n_keys, half = keys.shape
    T = qt.shape[1]
    tt = min(T, 256)
    out_f = jax.ShapeDtypeStruct((heads, n_keys, T), jnp.uint32)
    out_b = jax.ShapeDtypeStruct((heads, n_keys // 2, T), jnp.uint32)
    ospec = pl.BlockSpec((heads, n_keys, tt), lambda i: (0, 0, i))
    ospec_b = pl.BlockSpec((heads, n_keys // 2, tt), lambda i: (0, 0, i))
    return pl.pallas_call(
        functools.partial(_peer_topk_kernel, heads=heads, half=half, topk=PEER_TOPK),
        grid=(T // tt,),
        in_specs=[pl.BlockSpec((heads * 2 * half, tt), lambda i: (0, i)),
                  pl.BlockSpec((heads, 2, n_keys, half), lambda i: (0, 0, 0, 0))],
        out_specs=[ospec, ospec, ospec_b, ospec_b],
        out_shape=[out_f, out_f, out_b, out_b],
        compiler_params=_params("parallel"),
        name="peer_topk",
    )(qt, keys)


def _peer_gate_tile(a, cnt_rows, p1_rows, r2_ref, p2_ref, lanes, *, heads, n_keys):
    ab = a.astype(BF16)
    gelu = 0.5 * ab * (1.0 + lax.erf(ab * (2.0 ** -0.5)))
    gate = jnp.zeros((n_keys, LANE), BF16)
    for h in range(heads):
        cnt = pltpu.bitcast(jnp.broadcast_to(cnt_rows[h][:, lanes], (n_keys // 2, LANE)), BF16)
        p1 = pltpu.bitcast(jnp.broadcast_to(p1_rows[h][:, lanes], (n_keys // 2, LANE)), BF16)
        r2 = pltpu.bitcast(r2_ref[h, :, lanes], BF16)
        p2 = pltpu.bitcast(p2_ref[h, :, lanes], BF16)
        gate = gate + jnp.where(r2 < cnt, p2 * p1, jnp.zeros_like(gate))
    return (gate * gelu).T


def _peer_dense_step(blk, c_new, c_old, xt_ref, u_ref, v_ref, cnt1_ref, p1_ref, r2_ref, p2_ref, out_ref,
                     *, heads, n_keys, groups):
    tm = xt_ref.shape[1]
    dc = out_ref.shape[1] // groups
    for r in range(groups):
        if c_new is not None:
            e1 = blk * groups + r
            rows = slice(r * n_keys, (r + 1) * n_keys)
            a = _dot(u_ref[rows, :], xt_ref[...])
            cnt_rows = [cnt1_ref[h, pl.ds(e1, 1), :] for h in range(heads)]
            p1_rows = [p1_ref[h, pl.ds(e1, 1), :] for h in range(heads)]
            for lt in range(tm // LANE):
                lanes = slice(lt * LANE, (lt + 1) * LANE)
                c_new[lanes, rows] = _peer_gate_tile(a[:, lanes], cnt_rows, p1_rows, r2_ref, p2_ref, lanes,
                                                     heads=heads, n_keys=n_keys)
        if c_old is not None:
            cols = slice(r * dc, (r + 1) * dc)
            out_ref[:, cols] += _dot(c_old[...], v_ref[:, cols])


def _peer_dense_kernel(xt_ref, u_ref, v_ref, cnt1_ref, p1_ref, r2_ref, p2_ref, res_ref, gain_ref, *rest,
                       heads, n_keys, groups, nblk, final):
    if final:
        out_ref, c0_ref, c1_ref = rest
    else:
        out_ref, norm_ref, c0_ref, c1_ref = rest
    g = pl.program_id(1)
    step = functools.partial(_peer_dense_step, g, xt_ref=xt_ref, u_ref=u_ref, v_ref=v_ref,
                             cnt1_ref=cnt1_ref, p1_ref=p1_ref, r2_ref=r2_ref, p2_ref=p2_ref, out_ref=out_ref,
                             heads=heads, n_keys=n_keys, groups=groups)
    bufs = (c0_ref, c1_ref)
    inner = jnp.logical_and(g > 0, g < nblk)

    @pl.when(g == 0)
    def _():
        out_ref[...] = res_ref[...]
        step(bufs[0], None)

    @pl.when(jnp.logical_and(inner, lax.rem(g, 2) == 0))
    def _():
        step(bufs[0], bufs[1])

    @pl.when(jnp.logical_and(inner, lax.rem(g, 2) == 1))
    def _():
        step(bufs[1], bufs[0])

    @pl.when(g == nblk)
    def _():
        step(None, bufs[(nblk - 1) % 2])
        x = out_ref[...]
        y = x * lax.rsqrt(jnp.mean(x * x, axis=-1, keepdims=True) + EPS) * gain_ref[...]
        if final:
            out_ref[...] = y
        else:
            norm_ref[...] = y.astype(norm_ref.dtype)


def peer_dense(xt, u, v, cnt1, p1, r2, p2, residual, post_gain, final):
    D, T = xt.shape
    heads, n_keys, _ = cnt1.shape
    E = u.shape[0]
    groups = 4
    eb = groups * n_keys
    nblk = E // eb
    tm = min(T, 512)
    gspec = pl.BlockSpec((heads, n_keys, tm), lambda i, g: (0, 0, i))
    gspec_b = pl.BlockSpec((heads, n_keys // 2, tm), lambda i, g: (0, 0, i))
    row_spec = pl.BlockSpec((tm, D), lambda i, g: (i, 0))
    out_specs = row_spec if final else [row_spec, row_spec]
    out_shape = jax.ShapeDtypeStruct((T, D), F32)
    if not final:
        out_shape = [out_shape, jax.ShapeDtypeStruct((T, D), BF16)]
    return pl.pallas_call(
        functools.partial(_peer_dense_kernel, heads=heads, n_keys=n_keys, groups=groups, nblk=nblk, final=final),
        grid=(T // tm, nblk + 1),
        in_specs=[pl.BlockSpec((D, tm), lambda i, g: (0, i)),
                  pl.BlockSpec((eb, D), lambda i, g: (jnp.minimum(g, nblk - 1), 0)),
                  pl.BlockSpec((eb, D), lambda i, g: (jnp.maximum(g - 1, 0), 0)),
                  gspec, gspec, gspec_b, gspec_b,
                  pl.BlockSpec((tm, D), lambda i, g: (i, 0), pipeline_mode=pl.Buffered(1)),
                  pl.BlockSpec((1, D), lambda i, g: (0, 0))],
        out_specs=out_specs,
        out_shape=out_shape,
        scratch_shapes=[pltpu.VMEM((tm, eb), BF16), pltpu.VMEM((tm, eb), BF16)],
        compiler_params=_params("parallel", "arbitrary"),
        name="peer_dense",
    )(xt, u, v, cnt1, p1, r2, p2, residual, post_gain.reshape(1, D).astype(F32))


def _cast_kernel(x_ref, o_ref):
    o_ref[...] = x_ref[0].astype(o_ref.dtype)


def cast_table(tabs, layer, dtype):
    _, rows, cols = tabs.shape
    tr = min(rows, 1024)
    return pl.pallas_call(
        _cast_kernel,
        grid=(rows // tr,),
        in_specs=[pl.BlockSpec((1, tr, cols), lambda i: (layer, i, 0))],
        out_specs=pl.BlockSpec((tr, cols), lambda i: (i, 0)),
        out_shape=jax.ShapeDtypeStruct((rows, cols), dtype),
        compiler_params=_params("parallel"),
        name="cast_table",
    )(tabs)


def peer_layer(x, ht, w_q, sub_keys, u_tabs, v_tabs, layer, post_gain, final):
    qt = matmul(w_q.T.astype(BF16), ht, BF16, tn=512, name="peer_q")
    cnt1, p1, r2, p2 = peer_topk(qt, sub_keys.astype(BF16))
    return peer_dense(ht, cast_table(u_tabs, layer, BF16), cast_table(v_tabs, layer, BF16), cnt1, p1, r2, p2, x,
                      post_gain, final)


def _pad_cols(w, n):
    return jnp.pad(w, ((0, 0), (0, n - w.shape[1])))


def even_layer(x, h, batch, seq, w_in, b_igate, b_fgate, head_gain, w_pool, pool_scale, w_out, peer_gain):
    H = MLSTM_HEADS
    width = head_gain.shape[0]
    qk = width // 2
    pool_w = pool_scale.shape[0]
    n_main = 2 * qk + 2 * width
    w_cat = jnp.concatenate([w_in[:, :n_main], w_in[:, n_main + 2 * H:], w_in[:, n_main:n_main + 2 * H]], axis=1)
    tn = 3 * MXU_COLS
    z = matmul(h, _pad_cols(w_cat, -(-w_cat.shape[1] // tn) * tn).astype(BF16), F32, tn=tn, name="even_in")
    gate_bias = jnp.pad(jnp.concatenate([b_igate, b_fgate]), (0, LANE - 2 * H)).reshape(1, LANE).astype(F32)
    hm = mlstm_block(z, gate_bias, head_gain, batch, seq, 0, qk, 2 * qk, 2 * qk + width, n_main + pool_w)
    hp = pool_block(z, w_pool, pool_scale, batch, seq, n_main)
    return out_proj([hm, hp], w_out.astype(BF16), x, peer_gain)


def odd_layer(x, h, batch, seq, w_in, w_gate, b_gate, head_gain, w_out, peer_gain):
    rank, dkt = w_gate.shape
    dvt = head_gain.shape[0]
    n_main = 2 * dkt + 2 * dvt
    tn = 5 * MXU_COLS
    z = matmul(h, _pad_cols(w_in, -(-w_in.shape[1] // tn) * tn).astype(BF16), F32, tn=tn, name="odd_in")
    wg = jnp.pad(w_gate, ((0, LANE - rank), (0, 0))).astype(BF16)
    y = gla_block(z, wg, b_gate, head_gain, batch, seq, 0, dkt, 2 * dkt, 2 * dkt + dvt, n_main)
    return out_proj([y], w_out.astype(BF16), x, peer_gain)


def kernel(x, e_norm, e_w_in, e_b_igate, e_b_fgate, e_head_gain, e_w_pool, e_pool_scale, e_w_out, o_norm, o_w_in, o_w_gate, o_b_gate, o_head_gain, o_w_out, f_norm, f_w_q, f_sub_keys, f_u, f_v, final_norm):
    B, S, D = x.shape
    depth = f_norm.shape[0]

    def mixer_gain(layer):
        return e_norm[layer // 2] if layer % 2 == 0 else o_norm[layer // 2]

    xs = x.reshape(B * S, D)
    h = rmsnorm(xs, mixer_gain(0), BF16)
    for layer in range(depth):
        j = layer // 2
        if layer % 2 == 0:
            xs, ht = even_layer(xs, h, B, S, e_w_in[j], e_b_igate[j], e_b_fgate[j], e_head_gain[j],
                                e_w_pool[j], e_pool_scale[j], e_w_out[j], f_norm[layer])
        else:
            xs, ht = odd_layer(xs, h, B, S, o_w_in[j], o_w_gate[j], o_b_gate[j], o_head_gain[j], o_w_out[j],
                               f_norm[layer])
        if layer == depth - 1:
            return peer_layer(xs, ht, f_w_q[layer], f_sub_keys[layer], f_u, f_v, layer,
                              final_norm, True).reshape(B, S, D)
        xs, h = peer_layer(xs, ht, f_w_q[layer], f_sub_keys[layer], f_u, f_v, layer,
                           mixer_gain(layer + 1), False)
```

```python
import functools

import jax
import jax.numpy as jnp
from jax import lax
from jax.experimental import pallas as pl
from jax.experimental.pallas import tpu as pltpu

F32 = jnp.float32
BF16 = jnp.bfloat16
EPS = 1e-6
LANE = 128
SUBLANE = 8
MXU_COLS = 256
VMEM_LIMIT = 56 * 1024 * 1024

MLSTM_HEADS = 4
MLSTM_CHUNK = 64
POOL_WINDOWS = (2, 4, 8, 16)
POOL_HALO = 16
GLA_HEADS = 4
GLA_CHUNK = 32
GLA_TAU = 16.0
PEER_TOPK = 16
NEG_INF = float("-inf")
LOG2E = 1.4426950408889634

NT_DIMS = (((1,), (1,)), ((), ()))
TN_DIMS = (((0,), (0,)), ((), ()))


def _params(*sem):
    return pltpu.CompilerParams(dimension_semantics=sem, vmem_limit_bytes=VMEM_LIMIT)


def _log_sigmoid(x):
    return jnp.minimum(x, 0.0) - jnp.log1p(jnp.exp(-jnp.abs(x)))


def _sigmoid(x):
    return 1.0 / (1.0 + jnp.exp(-x))


def _dot(a, b, dims=None):
    if dims is None:
        return jnp.dot(a, b, preferred_element_type=F32)
    return lax.dot_general(a, b, dims, preferred_element_type=F32)


def _cumsum_rows(x, tri):
    hi = x.astype(BF16)
    lo = (x - hi.astype(F32)).astype(BF16)
    return _dot(tri, hi) + _dot(tri, lo)


def _tril(n, dtype):
    r = lax.broadcasted_iota(jnp.int32, (n, n), 0)
    c = lax.broadcasted_iota(jnp.int32, (n, n), 1)
    return (c <= r).astype(dtype)


def _rms_kernel(x_ref, g_ref, o_ref):
    x = x_ref[...]
    y = x * lax.rsqrt(jnp.mean(x * x, axis=-1, keepdims=True) + EPS) * g_ref[...]
    o_ref[...] = y.astype(o_ref.dtype)


def rmsnorm(x, g, out_dtype):
    T, D = x.shape
    tm = min(T, 512)
    return pl.pallas_call(
        _rms_kernel,
        grid=(T // tm,),
        in_specs=[pl.BlockSpec((tm, D), lambda i: (i, 0)), pl.BlockSpec((1, D), lambda i: (0, 0))],
        out_specs=pl.BlockSpec((tm, D), lambda i: (i, 0)),
        out_shape=jax.ShapeDtypeStruct((T, D), out_dtype),
        compiler_params=_params("parallel"),
        name="rmsnorm",
    )(x, g.reshape(1, D).astype(F32))


def _mm_kernel(a_ref, b_ref, o_ref):
    o_ref[...] = _dot(a_ref[...], b_ref[...]).astype(o_ref.dtype)


def matmul(a, b, out_dtype, tn, name="matmul"):
    M, K = a.shape
    N = b.shape[1]
    tm = min(M, 1024)
    tn = min(N, tn)
    return pl.pallas_call(
        _mm_kernel,
        grid=(M // tm, N // tn),
        in_specs=[pl.BlockSpec((tm, K), lambda i, j: (i, 0)), pl.BlockSpec((K, tn), lambda i, j: (0, j))],
        out_specs=pl.BlockSpec((tm, tn), lambda i, j: (i, j)),
        out_shape=jax.ShapeDtypeStruct((M, N), out_dtype),
        compiler_params=_params("parallel", "parallel"),
        name=name,
    )(a, b)


def _out_proj_kernel(*refs, n_parts):
    parts = refs[:n_parts]
    w_ref, res_ref, gain_ref, x_ref, nt_ref = refs[n_parts:]
    acc = res_ref[...]
    k0 = 0
    for p in parts:
        kp = p.shape[1]
        acc = acc + _dot(p[...], w_ref[k0:k0 + kp, :])
        k0 += kp
    x_ref[...] = acc
    y = acc * lax.rsqrt(jnp.mean(acc * acc, axis=-1, keepdims=True) + EPS) * gain_ref[...]
    nt_ref[...] = y.T.astype(nt_ref.dtype)


def out_proj(parts, w, residual, norm_gain):
    T, N = residual.shape
    K = w.shape[0]
    tm = min(T, 512)
    in_specs = [pl.BlockSpec((tm, p.shape[1]), lambda i: (i, 0)) for p in parts]
    in_specs += [pl.BlockSpec((K, N), lambda i: (0, 0), pipeline_mode=pl.Buffered(1)),
                 pl.BlockSpec((tm, N), lambda i: (i, 0)),
                 pl.BlockSpec((1, N), lambda i: (0, 0))]
    return pl.pallas_call(
        functools.partial(_out_proj_kernel, n_parts=len(parts)),
        grid=(T // tm,),
        in_specs=in_specs,
        out_specs=[pl.BlockSpec((tm, N), lambda i: (i, 0)), pl.BlockSpec((N, tm), lambda i: (0, i))],
        out_shape=[jax.ShapeDtypeStruct((T, N), F32), jax.ShapeDtypeStruct((N, T), BF16)],
        compiler_params=_params("parallel"),
        name="out_proj",
    )(*parts, w, residual, norm_gain.reshape(1, N).astype(F32))


def _mlstm_kernel(q_ref, k_ref, v_ref, o_ref, g_ref, gb_ref, gain_ref, out_ref, cn_ref, m_ref,
                  *, batch, heads, dk, dv, chunk, nchunks):
    L = chunk

    @pl.when(pl.program_id(0) == 0)
    def _():
        cn_ref[...] = jnp.zeros_like(cn_ref)
        m_ref[...] = jnp.zeros_like(m_ref)

    tri_b = _tril(L, BF16)
    causal = _tril(L, jnp.int32) > 0
    ones_col = (lax.broadcasted_iota(jnp.int32, (L, LANE), 1) == 0).astype(F32)
    kscale = dk ** -0.5

    chains = [(b, h) for b in range(batch) for h in range(heads)]
    n_chain = len(chains)

    def per_chain(fn):
        return jnp.stack([fn(b, h) for b, h in chains], axis=0)

    def chunk_body(c, carry):
        r0 = pl.multiple_of(c * L, L)
        rows = pl.ds(r0, L)
        g = [g_ref[b, rows, :] + gb_ref[...] for b in range(batch)]
        bc = [_cumsum_rows(_log_sigmoid(g[b]), tri_b) for b in range(batch)]
        g_t = [x.T for x in g]
        bc_t = [x.T for x in bc]
        i_col = per_chain(lambda b, h: g[b][:, h:h + 1])
        i_row = per_chain(lambda b, h: g_t[b][h:h + 1, :])
        b_col = per_chain(lambda b, h: bc[b][:, heads + h:heads + h + 1])
        b_row = per_chain(lambda b, h: bc_t[b][heads + h:heads + h + 1, :])
        m_prev = m_ref[:, 0:1, 0:1]

        dmat = jnp.where(causal, b_col - b_row + i_row, NEG_INF)
        inter = b_col + m_prev
        m_t = jnp.maximum(jnp.max(dmat, axis=-1, keepdims=True), inter)
        a_inter = jnp.exp(inter - m_t)
        decay = jnp.exp(dmat - m_t)
        b_last = b_col[:, L - 1:L, :]
        dec = b_last - b_col + i_col
        m_new = jnp.maximum(b_last + m_prev, jnp.max(dec, axis=1, keepdims=True))
        wk = jnp.exp(dec - m_new)
        sc = jnp.exp(b_last + m_prev - m_new)

        qb = [q_ref[b, rows, h * dk:(h + 1) * dk].astype(BF16) for b, h in chains]
        kb = [(k_ref[b, rows, h * dk:(h + 1) * dk] * kscale).astype(BF16) for b, h in chains]
        v_aug = per_chain(lambda b, h: jnp.concatenate([v_ref[b, rows, h * dv:(h + 1) * dv], ones_col], axis=1))
        v_aug_b = v_aug.astype(BF16)
        s = (jnp.stack([_dot(qb[i], kb[i], NT_DIMS) for i in range(n_chain)], axis=0) * decay).astype(BF16)
        intra = jnp.stack([_dot(s[i], v_aug_b[i]) for i in range(n_chain)], axis=0)
        inter_out = jnp.stack([_dot(qb[i], cn_ref[i].astype(BF16), NT_DIMS) for i in range(n_chain)], axis=0)
        hn = intra + a_inter * inter_out

        wv = (wk * v_aug).astype(BF16)
        cn_ref[...] = sc * cn_ref[...] + jnp.stack([_dot(wv[i], kb[i], TN_DIMS) for i in range(n_chain)], axis=0)
        m_ref[...] = jnp.broadcast_to(m_new, m_ref.shape)

        hh = hn[:, :, :dv] / jnp.maximum(jnp.abs(hn[:, :, dv:dv + 1]), jnp.exp(-m_t))
        gain = per_chain(lambda b, h: gain_ref[:, h * dv:(h + 1) * dv])
        o_gate = _sigmoid(per_chain(lambda b, h: o_ref[b, rows, h * dv:(h + 1) * dv]))
        y = (hh * lax.rsqrt(jnp.mean(hh * hh, axis=-1, keepdims=True) + EPS) * gain * o_gate).astype(out_ref.dtype)
        for i, (b, h) in enumerate(chains):
            out_ref[b, rows, h * dv:(h + 1) * dv] = y[i]
        return carry

    lax.fori_loop(0, nchunks, chunk_body, 0)


def mlstm_block(z, gate_bias, head_gain, batch, seq, q_off, k_off, v_off, o_off, g_off):
    T, cols = z.shape
    H, L = MLSTM_HEADS, MLSTM_CHUNK
    width = head_gain.shape[0]
    dv = width // H
    dk = dv // 2
    ts = min(seq, 256)
    z3 = z.reshape(batch, seq, cols)

    def col(off, w):
        assert off % w == 0
        return pl.BlockSpec((batch, ts, w), lambda s: (0, s, off // w))

    n_state = batch * H
    out = pl.pallas_call(
        functools.partial(_mlstm_kernel, batch=batch, heads=H, dk=dk, dv=dv, chunk=L, nchunks=ts // L),
        grid=(seq // ts,),
        in_specs=[col(q_off, H * dk), col(k_off, H * dk), col(v_off, width), col(o_off, width), col(g_off, LANE),
                  pl.BlockSpec((1, LANE), lambda s: (0, 0)), pl.BlockSpec((1, width), lambda s: (0, 0))],
        out_specs=pl.BlockSpec((batch, ts, width), lambda s: (0, s, 0)),
        out_shape=jax.ShapeDtypeStruct((batch, seq, width), BF16),
        scratch_shapes=[pltpu.VMEM((n_state, dv + LANE, dk), F32),
                        pltpu.VMEM((n_state, SUBLANE, LANE), F32)],
        compiler_params=_params("arbitrary"),
        name="mlstm",
    )(z3, z3, z3, z3, z3, gate_bias, head_gain.reshape(1, width).astype(F32))
    return out.reshape(T, width)


def _pool_kernel(p_ref, w_ref, s_ref, out_ref, halo_ref, *, group, ts):
    si = pl.program_id(1)

    @pl.when(si == 0)
    def _():
        halo_ref[...] = jnp.zeros_like(halo_ref)

    p = p_ref[...]
    buf = jnp.concatenate([halo_ref[...], p], axis=0)
    t_abs = si * ts + lax.broadcasted_iota(jnp.int32, (ts, 1), 0)
    win = buf
    span = 1
    for g, w in enumerate(POOL_WINDOWS):
        while span < w:
            win = win + pltpu.roll(win, span, 0)
            span *= 2
        sl = slice(g * group, (g + 1) * group)
        cnt = jnp.minimum(t_abs + 1, w).astype(F32)
        mixed = win[POOL_HALO:, sl] / cnt - p[:, sl]
        y = _dot(mixed.astype(BF16), w_ref[g]) * s_ref[:, sl]
        out_ref[:, sl] = y.astype(out_ref.dtype)
    halo_ref[...] = p[ts - POOL_HALO:, :]


def pool_block(z, w_pool, pool_scale, batch, seq, p_off):
    T = z.shape[0]
    ng, group, _ = w_pool.shape
    width = ng * group
    ts = min(seq, 512)
    nblk = seq // ts
    assert p_off % width == 0
    return pl.pallas_call(
        functools.partial(_pool_kernel, group=group, ts=ts),
        grid=(batch, nblk),
        in_specs=[pl.BlockSpec((ts, width), lambda b, s: (b * nblk + s, p_off // width)),
                  pl.BlockSpec((ng, group, group), lambda b, s: (0, 0, 0)),
                  pl.BlockSpec((1, width), lambda b, s: (0, 0))],
        out_specs=pl.BlockSpec((ts, width), lambda b, s: (b * nblk + s, 0)),
        out_shape=jax.ShapeDtypeStruct((T, width), BF16),
        scratch_shapes=[pltpu.VMEM((POOL_HALO, width), F32)],
        compiler_params=_params("parallel", "arbitrary"),
        name="pool",
    )(z, w_pool.astype(BF16), pool_scale.reshape(1, width).astype(F32))


def _gla_kernel(q_ref, k_ref, v_ref, r_ref, gl_ref, wg_ref, bg_ref, gain_ref, out_ref, st_ref,
                *, batch, heads, dk, dv, chunk, nchunks):
    L = chunk
    Lh = L // 2

    @pl.when(pl.program_id(0) == 0)
    def _():
        st_ref[...] = jnp.zeros_like(st_ref)

    tri_b = _tril(L, BF16)
    row_id = lax.broadcasted_iota(jnp.int32, (L, 1), 0)
    row_half = lax.broadcasted_iota(jnp.int32, (Lh, 1), 0)
    lane_id = lax.broadcasted_iota(jnp.int32, (Lh, L), 1)
    qscale = dk ** -0.5
    chains = [(b, h) for b in range(batch) for h in range(heads)]
    n_chain = len(chains)

    def per_chain(fn):
        return jnp.stack([fn(b, h) for b, h in chains], axis=0)

    def chunk_body(c, carry):
        r0 = pl.multiple_of(c * L, L)
        rows = pl.ds(r0, L)
        b_seq = []
        for b in range(batch):
            la = _log_sigmoid(_dot(gl_ref[b, rows, :].astype(BF16), wg_ref[...]) + bg_ref[...]) / GLA_TAU
            b_seq.append(_cumsum_rows(la, tri_b))
        bb = per_chain(lambda b, h: b_seq[b][:, h * dk:(h + 1) * dk]) * LOG2E
        q = per_chain(lambda b, h: q_ref[b, rows, h * dk:(h + 1) * dk]) * qscale
        k = per_chain(lambda b, h: k_ref[b, rows, h * dk:(h + 1) * dk])

        att_half = [jnp.zeros((n_chain, Lh, L), F32), jnp.zeros((n_chain, Lh, L), F32)]
        for s in range(L):
            hf = s // Lh
            h0 = hf * Lh
            rel = jnp.where(row_half >= s - h0, bb[:, h0:h0 + Lh, :] - bb[:, s:s + 1, :], NEG_INF)
            col = jnp.sum(q[:, h0:h0 + Lh, :] * k[:, s:s + 1, :] * jnp.exp2(rel), axis=-1, keepdims=True)
            att_half[hf] = jnp.where(lane_id == s, col, att_half[hf])
        anchor = bb[:, Lh - 1:Lh, :]
        qd = (q * jnp.exp2(jnp.where(row_id >= Lh, bb - anchor, NEG_INF))).astype(BF16)
        kd = (k * jnp.exp2(jnp.where(row_id < Lh, anchor - bb, NEG_INF))).astype(BF16)
        att = jnp.concatenate(att_half, axis=1) + jnp.stack([_dot(qd[i], kd[i], NT_DIMS) for i in range(n_chain)], axis=0)
        att = att.astype(BF16)

        b_last = bb[:, L - 1:L, :]
        q_in = (q * jnp.exp2(bb)).astype(BF16)
        k_out = (k * jnp.exp2(b_last - bb)).astype(BF16)
        vb = [v_ref[b, rows, h * dv:(h + 1) * dv].astype(BF16) for b, h in chains]
        o = jnp.stack([_dot(att[i], vb[i]) + _dot(q_in[i], st_ref[i].astype(BF16), NT_DIMS) for i in range(n_chain)], axis=0)
        st_ref[...] = jnp.exp2(b_last) * st_ref[...] + jnp.stack([_dot(vb[i], k_out[i], TN_DIMS) for i in range(n_chain)], axis=0)

        gain = per_chain(lambda b, h: gain_ref[:, h * dv:(h + 1) * dv])
        r = per_chain(lambda b, h: r_ref[b, rows, h * dv:(h + 1) * dv])
        y = o * lax.rsqrt(jnp.mean(o * o, axis=-1, keepdims=True) + EPS) * gain * (r * _sigmoid(r))
        y = y.astype(out_ref.dtype)
        for i, (b, h) in enumerate(chains):
            out_ref[b, rows, h * dv:(h + 1) * dv] = y[i]
        return carry

    lax.fori_loop(0, nchunks, chunk_body, 0)


def gla_block(z, w_gate, b_gate, head_gain, batch, seq, q_off, k_off, v_off, r_off, g_off):
    T, cols = z.shape
    H, L = GLA_HEADS, GLA_CHUNK
    dkt = w_gate.shape[1]
    dvt = head_gain.shape[0]
    dk, dv = dkt // H, dvt // H
    ts = min(seq, 128)
    z3 = z.reshape(batch, seq, cols)

    def col(off, w):
        assert off % w == 0
        return pl.BlockSpec((batch, ts, w), lambda s: (0, s, off // w))

    out = pl.pallas_call(
        functools.partial(_gla_kernel, batch=batch, heads=H, dk=dk, dv=dv, chunk=L, nchunks=ts // L),
        grid=(seq // ts,),
        in_specs=[col(q_off, dkt), col(k_off, dkt), col(v_off, dvt), col(r_off, dvt), col(g_off, LANE),
                  pl.BlockSpec((LANE, dkt), lambda s: (0, 0)),
                  pl.BlockSpec((1, dkt), lambda s: (0, 0)),
                  pl.BlockSpec((1, dvt), lambda s: (0, 0))],
        out_specs=pl.BlockSpec((batch, ts, dvt), lambda s: (0, s, 0)),
        out_shape=jax.ShapeDtypeStruct((batch, seq, dvt), BF16),
        scratch_shapes=[pltpu.VMEM((batch * H, dv, dk), F32)],
        compiler_params=_params("arbitrary"),
        name="gla",
    )(z3, z3, z3, z3, z3, w_gate, b_gate.reshape(1, dkt).astype(F32), head_gain.reshape(1, dvt).astype(F32))
    return out.reshape(T, dvt)


def _extract_topk(vals, k):
    rank = jnp.full(vals.shape, float(k), F32)
    tops = []
    for i in range(k):
        mx = jnp.max(vals, axis=0, keepdims=True)
        hit = vals == mx
        rank = jnp.where(hit, float(i), rank)
        vals = jnp.where(hit, NEG_INF, vals)
        tops.append(mx)
    return jnp.concatenate(tops, axis=0), rank


def _dup_bf16(x):
    hi = pltpu.bitcast(x.astype(BF16).astype(F32), jnp.uint32)
    return hi | lax.shift_right_logical(hi, jnp.uint32(16))


def _peer_topk_kernel(qt_ref, keys_ref, cnt1_ref, p1_ref, r2_ref, p2_ref, *, heads, half, topk):
    K = topk
    for h in range(heads):
        s1 = _dot(keys_ref[h, 0], qt_ref[(2 * h) * half:(2 * h + 1) * half, :])
        s2 = _dot(keys_ref[h, 1], qt_ref[(2 * h + 1) * half:(2 * h + 2) * half, :])
        sv1, rank1 = _extract_topk(s1, K)
        sv2, rank2 = _extract_topk(s2, K)
        cand = []
        a = 0
        while K // (a + 1) > 1:
            nb = K // (a + 1)
            rows = -(-nb // SUBLANE) * SUBLANE
            c = sv1[a:a + 1, :] + sv2[:rows, :]
            if nb < rows:
                c = jnp.where(lax.broadcasted_iota(jnp.int32, (rows, 1), 0) < nb, c, NEG_INF)
            cand.append(c)
            a += 1
        n_single = a
        cand.append(sv1[n_single:, :] + sv2[0:1, :])
        work = jnp.concatenate(cand, axis=0)
        tau = None
        for _ in range(K):
            tau = jnp.max(work, axis=0, keepdims=True)
            work = jnp.where(work == tau, NEG_INF, work)
        cmax = sv1[0:1, :] + sv2[0:1, :]
        zsum = jnp.zeros_like(cmax)
        cnt1 = jnp.zeros_like(s1)
        for a, c in enumerate(cand):
            sel = c >= tau
            zsum = zsum + jnp.sum(jnp.where(sel, jnp.exp(c - cmax), 0.0), axis=0, keepdims=True)
            if a < n_single:
                cnt1 = jnp.where(rank1 == float(a), jnp.sum(sel.astype(F32), axis=0, keepdims=True), cnt1)
            else:
                for j in range(K - n_single):
                    cnt1 = jnp.where(rank1 == float(a + j), sel[j:j + 1, :].astype(F32), cnt1)
        cnt1_ref[h] = _dup_bf16(cnt1)
        p1_ref[h] = _dup_bf16(jnp.exp(s1 - sv1[0:1, :]) / zsum)
        r2_ref[h] = pltpu.bitcast(rank2.astype(BF16), jnp.uint32)
        p2_ref[h] = pltpu.bitcast(jnp.exp(s2 - sv2[0:1, :]).astype(BF16), jnp.uint32)


def peer_topk(qt, keys):
    heads, _, n_keys, half = keys.shape
    T = qt.shape[1]
    tt = min(T, 256)
    out_f = jax.ShapeDtypeStruct((heads, n_keys, T), jnp.uint32)
    out_b = jax.ShapeDtypeStruct((heads, n_keys // 2, T), jnp.uint32)
    ospec = pl.BlockSpec((heads, n_keys, tt), lambda i: (0, 0, i))
    ospec_b = pl.BlockSpec((heads, n_keys // 2, tt), lambda i: (0, 0, i))
    return pl.pallas_call(
        functools.partial(_peer_topk_kernel, heads=heads, half=half, topk=PEER_TOPK),
        grid=(T // tt,),
        in_specs=[pl.BlockSpec((heads * 2 * half, tt), lambda i: (0, i)),
                  pl.BlockSpec((heads, 2, n_keys, half), lambda i: (0, 0, 0, 0))],
        out_specs=[ospec, ospec, ospec_b, ospec_b],
        out_shape=[out_f, out_f, out_b, out_b],
        compiler_params=_params("parallel"),
        name="peer_topk",
    )(qt, keys)


def _peer_gate_tile(a, cnt_rows, p1_rows, r2_ref, p2_ref, lanes, *, heads, n_keys):
    ab = a.astype(BF16)
    gelu = 0.5 * ab * (1.0 + lax.erf(ab * (2.0 ** -0.5)))
    gate = jnp.zeros((n_keys, LANE), BF16)
    for h in range(heads):
        cnt = pltpu.bitcast(jnp.broadcast_to(cnt_rows[h][:, lanes], (n_keys // 2, LANE)), BF16)
        p1 = pltpu.bitcast(jnp.broadcast_to(p1_rows[h][:, lanes], (n_keys // 2, LANE)), BF16)
        r2 = pltpu.bitcast(r2_ref[h, :, lanes], BF16)
        p2 = pltpu.bitcast(p2_ref[h, :, lanes], BF16)
        gate = gate + jnp.where(r2 < cnt, p2 * p1, jnp.zeros_like(gate))
    return (gate * gelu).T


def _peer_dense_step(blk, c_new, c_old, xt_ref, u_ref, v_ref, cnt1_ref, p1_ref, r2_ref, p2_ref, out_ref,
                     *, heads, n_keys, groups):
    tm = xt_ref.shape[1]
    dc = out_ref.shape[1] // groups
    for r in range(groups):
        if c_new is not None:
            e1 = blk * groups + r
            rows = slice(r * n_keys, (r + 1) * n_keys)
            a = _dot(u_ref[rows, :], xt_ref[...])
            cnt_rows = [cnt1_ref[h, pl.ds(e1, 1), :] for h in range(heads)]
            p1_rows = [p1_ref[h, pl.ds(e1, 1), :] for h in range(heads)]
            for lt in range(tm // LANE):
                lanes = slice(lt * LANE, (lt + 1) * LANE)
                c_new[lanes, rows] = _peer_gate_tile(a[:, lanes], cnt_rows, p1_rows, r2_ref, p2_ref, lanes,
                                                     heads=heads, n_keys=n_keys)
        if c_old is not None:
            cols = slice(r * dc, (r + 1) * dc)
            out_ref[:, cols] += _dot(c_old[...], v_ref[:, cols])


def _peer_dense_kernel(xt_ref, u_ref, v_ref, cnt1_ref, p1_ref, r2_ref, p2_ref, res_ref, gain_ref, *rest,
                       heads, n_keys, groups, nblk, final):
    if final:
        out_ref, c0_ref, c1_ref = rest
    else:
        out_ref, norm_ref, c0_ref, c1_ref = rest
    g = pl.program_id(1)
    step = functools.partial(_peer_dense_step, g, xt_ref=xt_ref, u_ref=u_ref, v_ref=v_ref,
                             cnt1_ref=cnt1_ref, p1_ref=p1_ref, r2_ref=r2_ref, p2_ref=p2_ref, out_ref=out_ref,
                             heads=heads, n_keys=n_keys, groups=groups)
    bufs = (c0_ref, c1_ref)
    inner = jnp.logical_and(g > 0, g < nblk)

    @pl.when(g == 0)
    def _():
        out_ref[...] = res_ref[...]
        step(bufs[0], None)

    @pl.when(jnp.logical_and(inner, lax.rem(g, 2) == 0))
    def _():
        step(bufs[0], bufs[1])

    @pl.when(jnp.logical_and(inner, lax.rem(g, 2) == 1))
    def _():
        step(bufs[1], bufs[0])

    @pl.when(g == nblk)
    def _():
        step(None, bufs[(nblk - 1) % 2])
        x = out_ref[...]
        y = x * lax.rsqrt(jnp.mean(x * x, axis=-1, keepdims=True) + EPS) * gain_ref[...]
        if final:
            out_ref[...] = y
        else:
            norm_ref[...] = y.astype(norm_ref.dtype)


def peer_dense(xt, u, v, cnt1, p1, r2, p2, residual, post_gain, final):
    D, T = xt.shape
    heads, n_keys, _ = cnt1.shape
    E = u.shape[0]
    groups = 8
    eb = groups * n_keys
    nblk = E // eb
    tm = min(T, 512)
    gspec = pl.BlockSpec((heads, n_keys, tm), lambda i, g: (0, 0, i))
    gspec_b = pl.BlockSpec((heads, n_keys // 2, tm), lambda i, g: (0, 0, i))
    row_spec = pl.BlockSpec((tm, D), lambda i, g: (i, 0))
    out_specs = row_spec if final else [row_spec, row_spec]
    out_shape = jax.ShapeDtypeStruct((T, D), F32)
    if not final:
        out_shape = [out_shape, jax.ShapeDtypeStruct((T, D), BF16)]
    return pl.pallas_call(
        functools.partial(_peer_dense_kernel, heads=heads, n_keys=n_keys, groups=groups, nblk=nblk, final=final),
        grid=(T // tm, nblk + 1),
        in_specs=[pl.BlockSpec((D, tm), lambda i, g: (0, i)),
                  pl.BlockSpec((eb, D), lambda i, g: (jnp.minimum(g, nblk - 1), 0)),
                  pl.BlockSpec((eb, D), lambda i, g: (jnp.maximum(g - 1, 0), 0)),
                  gspec, gspec, gspec_b, gspec_b,
                  pl.BlockSpec((tm, D), lambda i, g: (i, 0), pipeline_mode=pl.Buffered(1)),
                  pl.BlockSpec((1, D), lambda i, g: (0, 0))],
        out_specs=out_specs,
        out_shape=out_shape,
        scratch_shapes=[pltpu.VMEM((tm, eb), BF16), pltpu.VMEM((tm, eb), BF16)],
        compiler_params=_params("parallel", "arbitrary"),
        name="peer_dense",
    )(xt, u, v, cnt1, p1, r2, p2, residual, post_gain.reshape(1, D).astype(F32))


def _cast_kernel(x_ref, o_ref):
    o_ref[...] = x_ref[0].astype(o_ref.dtype)


def cast_table(tabs, layer, dtype):
    _, rows, cols = tabs.shape
    tr = min(rows, 1024)
    return pl.pallas_call(
        _cast_kernel,
        grid=(rows // tr,),
        in_specs=[pl.BlockSpec((1, tr, cols), lambda i: (layer, i, 0))],
        out_specs=pl.BlockSpec((tr, cols), lambda i: (i, 0)),
        out_shape=jax.ShapeDtypeStruct((rows, cols), dtype),
        compiler_params=_params("parallel"),
        name="cast_table",
    )(tabs)


def peer_layer(x, ht, w_q, sub_keys, u_tabs, v_tabs, layer, post_gain, final):
    qt = matmul(w_q.T.astype(BF16), ht, BF16, tn=512, name="peer_q")
    cnt1, p1, r2, p2 = peer_topk(qt, sub_keys.astype(BF16))
    return peer_dense(ht, cast_table(u_tabs, layer, BF16), cast_table(v_tabs, layer, BF16), cnt1, p1, r2, p2, x,
                      post_gain, final)


def _pad_cols(w, n):
    return jnp.pad(w, ((0, 0), (0, n - w.shape[1])))


def even_layer(x, h, batch, seq, w_in, b_igate, b_fgate, head_gain, w_pool, pool_scale, w_out, peer_gain):
    H = MLSTM_HEADS
    width = head_gain.shape[0]
    qk = width // 2
    pool_w = pool_scale.shape[0]
    n_main = 2 * qk + 2 * width
    w_cat = jnp.concatenate([w_in[:, :n_main], w_in[:, n_main + 2 * H:], w_in[:, n_main:n_main + 2 * H]], axis=1)
    tn = 3 * MXU_COLS
    z = matmul(h, _pad_cols(w_cat, -(-w_cat.shape[1] // tn) * tn).astype(BF16), F32, tn=tn, name="even_in")
    gate_bias = jnp.pad(jnp.concatenate([b_igate, b_fgate]), (0, LANE - 2 * H)).reshape(1, LANE).astype(F32)
    hm = mlstm_block(z, gate_bias, head_gain, batch, seq, 0, qk, 2 * qk, 2 * qk + width, n_main + pool_w)
    hp = pool_block(z, w_pool, pool_scale, batch, seq, n_main)
    return out_proj([hm, hp], w_out.astype(BF16), x, peer_gain)


def odd_layer(x, h, batch, seq, w_in, w_gate, b_gate, head_gain, w_out, peer_gain):
    rank, dkt = w_gate.shape
    dvt = head_gain.shape[0]
    n_main = 2 * dkt + 2 * dvt
    tn = 5 * MXU_COLS
    z = matmul(h, _pad_cols(w_in, -(-w_in.shape[1] // tn) * tn).astype(BF16), F32, tn=tn, name="odd_in")
    wg = jnp.pad(w_gate, ((0, LANE - rank), (0, 0))).astype(BF16)
    y = gla_block(z, wg, b_gate, head_gain, batch, seq, 0, dkt, 2 * dkt, 2 * dkt + dvt, n_main)
    return out_proj([y], w_out.astype(BF16), x, peer_gain)


def kernel(x, e_norm, e_w_in, e_b_igate, e_b_fgate, e_head_gain, e_w_pool, e_pool_scale, e_w_out, o_norm, o_w_in, o_w_gate, o_b_gate, o_head_gain, o_w_out, f_norm, f_w_q, f_sub_keys, f_u, f_v, final_norm):
    B, S, D = x.shape
    depth = f_norm.shape[0]

    def mixer_gain(layer):
        return e_norm[layer // 2] if layer % 2 == 0 else o_norm[layer // 2]

    xs = x.reshape(B * S, D)
    h = rmsnorm(xs, mixer_gain(0), BF16)
    for layer in range(depth):
        j = layer // 2
        if layer % 2 == 0:
            xs, ht = even_layer(xs, h, B, S, e_w_in[j], e_b_igate[j], e_b_fgate[j], e_head_gain[j],
                                e_w_pool[j], e_pool_scale[j], e_w_out[j], f_norm[layer])
        else:
            xs, ht = odd_layer(xs, h, B, S, o_w_in[j], o_w_gate[j], o_b_gate[j], o_head_gain[j], o_w_out[j],
                               f_norm[layer])
        if layer == depth - 1:
            return peer_layer(xs, ht, f_w_q[layer], f_sub_keys[layer], f_u, f_v, layer,
                              final_norm, True).reshape(B, S, D)
        xs, h = peer_layer(xs, ht, f_w_q[layer], f_sub_keys[layer], f_u, f_v, layer,
                           mixer_gain(layer + 1), False)
```

```python
import functools

import jax
import jax.numpy as jnp
from jax import lax
from jax.experimental import pallas as pl
from jax.experimental.pallas import tpu as pltpu

F32 = jnp.float32
BF16 = jnp.bfloat16
EPS = 1e-6
LANE = 128
SUBLANE = 8
MXU_COLS = 256
VMEM_LIMIT = 56 * 1024 * 1024

MLSTM_HEADS = 4
MLSTM_CHUNK = 64
POOL_WINDOWS = (2, 4, 8, 16)
POOL_HALO = 16
GLA_HEADS = 4
GLA_CHUNK = 32
GLA_TAU = 16.0
PEER_TOPK = 16
NEG_INF = float("-inf")
LOG2E = 1.4426950408889634

NT_DIMS = (((1,), (1,)), ((), ()))
TN_DIMS = (((0,), (0,)), ((), ()))


def _params(*sem):
    return pltpu.CompilerParams(dimension_semantics=sem, vmem_limit_bytes=VMEM_LIMIT)


def _log_sigmoid(x):
    return jnp.minimum(x, 0.0) - jnp.log1p(jnp.exp(-jnp.abs(x)))


def _sigmoid(x):
    return 1.0 / (1.0 + jnp.exp(-x))


def _dot(a, b, dims=None):
    if dims is None:
        return jnp.dot(a, b, preferred_element_type=F32)
    return lax.dot_general(a, b, dims, preferred_element_type=F32)


def _cumsum_rows(x, tri):
    hi = x.astype(BF16)
    lo = (x - hi.astype(F32)).astype(BF16)
    return _dot(tri, hi) + _dot(tri, lo)


def _tril(n, dtype):
    r = lax.broadcasted_iota(jnp.int32, (n, n), 0)
    c = lax.broadcasted_iota(jnp.int32, (n, n), 1)
    return (c <= r).astype(dtype)


def _rms_kernel(x_ref, g_ref, o_ref):
    x = x_ref[...]
    y = x * lax.rsqrt(jnp.mean(x * x, axis=-1, keepdims=True) + EPS) * g_ref[...]
    o_ref[...] = y.astype(o_ref.dtype)


def rmsnorm(x, g, out_dtype):
    T, D = x.shape
    tm = min(T, 512)
    return pl.pallas_call(
        _rms_kernel,
        grid=(T // tm,),
        in_specs=[pl.BlockSpec((tm, D), lambda i: (i, 0)), pl.BlockSpec((1, D), lambda i: (0, 0))],
        out_specs=pl.BlockSpec((tm, D), lambda i: (i, 0)),
        out_shape=jax.ShapeDtypeStruct((T, D), out_dtype),
        compiler_params=_params("parallel"),
        name="rmsnorm",
    )(x, g.reshape(1, D).astype(F32))


def _mm_kernel(a_ref, b_ref, o_ref):
    o_ref[...] = _dot(a_ref[...], b_ref[...]).astype(o_ref.dtype)


def matmul(a, b, out_dtype, tn, name="matmul"):
    M, K = a.shape
    N = b.shape[1]
    tm = min(M, 1024)
    tn = min(N, tn)
    return pl.pallas_call(
        _mm_kernel,
        grid=(M // tm, N // tn),
        in_specs=[pl.BlockSpec((tm, K), lambda i, j: (i, 0)), pl.BlockSpec((K, tn), lambda i, j: (0, j))],
        out_specs=pl.BlockSpec((tm, tn), lambda i, j: (i, j)),
        out_shape=jax.ShapeDtypeStruct((M, N), out_dtype),
        compiler_params=_params("parallel", "parallel"),
        name=name,
    )(a, b)


def _out_proj_kernel(*refs, n_parts):
    parts = refs[:n_parts]
    w_ref, res_ref, gain_ref, x_ref, nt_ref = refs[n_parts:]
    acc = res_ref[...]
    k0 = 0
    for p in parts:
        kp = p.shape[1]
        acc = acc + _dot(p[...], w_ref[k0:k0 + kp, :])
        k0 += kp
    x_ref[...] = acc
    y = acc * lax.rsqrt(jnp.mean(acc * acc, axis=-1, keepdims=True) + EPS) * gain_ref[...]
    nt_ref[...] = y.T.astype(nt_ref.dtype)


def out_proj(parts, w, residual, norm_gain):
    T, N = residual.shape
    K = w.shape[0]
    tm = min(T, 512)
    in_specs = [pl.BlockSpec((tm, p.shape[1]), lambda i: (i, 0)) for p in parts]
    in_specs += [pl.BlockSpec((K, N), lambda i: (0, 0), pipeline_mode=pl.Buffered(1)),
                 pl.BlockSpec((tm, N), lambda i: (i, 0)),
                 pl.BlockSpec((1, N), lambda i: (0, 0))]
    return pl.pallas_call(
        functools.partial(_out_proj_kernel, n_parts=len(parts)),
        grid=(T // tm,),
        in_specs=in_specs,
        out_specs=[pl.BlockSpec((tm, N), lambda i: (i, 0)), pl.BlockSpec((N, tm), lambda i: (0, i))],
        out_shape=[jax.ShapeDtypeStruct((T, N), F32), jax.ShapeDtypeStruct((N, T), BF16)],
        compiler_params=_params("parallel"),
        name="out_proj",
    )(*parts, w, residual, norm_gain.reshape(1, N).astype(F32))


def _mlstm_kernel(q_ref, k_ref, v_ref, o_ref, g_ref, gb_ref, gain_ref, out_ref, cn_ref, m_ref,
                  *, batch, heads, dk, dv, chunk, nchunks):
    L = chunk

    @pl.when(pl.program_id(0) == 0)
    def _():
        cn_ref[...] = jnp.zeros_like(cn_ref)
        m_ref[...] = jnp.zeros_like(m_ref)

    tri_b = _tril(L, BF16)
    causal = _tril(L, jnp.int32) > 0
    ones_col = (lax.broadcasted_iota(jnp.int32, (L, LANE), 1) == 0).astype(F32)
    kscale = dk ** -0.5

    chains = [(b, h) for b in range(batch) for h in range(heads)]
    n_chain = len(chains)

    def per_chain(fn):
        return jnp.stack([fn(b, h) for b, h in chains], axis=0)

    def chunk_body(c, carry):
        r0 = pl.multiple_of(c * L, L)
        rows = pl.ds(r0, L)
        g = [g_ref[b, rows, :] + gb_ref[...] for b in range(batch)]
        bc = [_cumsum_rows(_log_sigmoid(g[b]), tri_b) for b in range(batch)]
        g_t = [x.T for x in g]
        bc_t = [x.T for x in bc]
        i_col = per_chain(lambda b, h: g[b][:, h:h + 1])
        i_row = per_chain(lambda b, h: g_t[b][h:h + 1, :])
        b_col = per_chain(lambda b, h: bc[b][:, heads + h:heads + h + 1])
        b_row = per_chain(lambda b, h: bc_t[b][heads + h:heads + h + 1, :])
        m_prev = m_ref[:, 0:1, 0:1]

        dmat = jnp.where(causal, b_col - b_row + i_row, NEG_INF)
        inter = b_col + m_prev
        m_t = jnp.maximum(jnp.max(dmat, axis=-1, keepdims=True), inter)
        a_inter = jnp.exp(inter - m_t)
        decay = jnp.exp(dmat - m_t)
        b_last = b_col[:, L - 1:L, :]
        dec = b_last - b_col + i_col
        m_new = jnp.maximum(b_last + m_prev, jnp.max(dec, axis=1, keepdims=True))
        wk = jnp.exp(dec - m_new)
        sc = jnp.exp(b_last + m_prev - m_new)

        qb = [q_ref[b, rows, h * dk:(h + 1) * dk].astype(BF16) for b, h in chains]
        kb = [(k_ref[b, rows, h * dk:(h + 1) * dk] * kscale).astype(BF16) for b, h in chains]
        v_aug = per_chain(lambda b, h: jnp.concatenate([v_ref[b, rows, h * dv:(h + 1) * dv], ones_col], axis=1))
        v_aug_b = v_aug.astype(BF16)
        s = (jnp.stack([_dot(qb[i], kb[i], NT_DIMS) for i in range(n_chain)], axis=0) * decay).astype(BF16)
        intra = jnp.stack([_dot(s[i], v_aug_b[i]) for i in range(n_chain)], axis=0)
        inter_out = jnp.stack([_dot(qb[i], cn_ref[i].astype(BF16), NT_DIMS) for i in range(n_chain)], axis=0)
        hn = intra + a_inter * inter_out

        wv = (wk * v_aug).astype(BF16)
        cn_ref[...] = sc * cn_ref[...] + jnp.stack([_dot(wv[i], kb[i], TN_DIMS) for i in range(n_chain)], axis=0)
        m_ref[...] = jnp.broadcast_to(m_new, m_ref.shape)

        hh = hn[:, :, :dv] / jnp.maximum(jnp.abs(hn[:, :, dv:dv + 1]), jnp.exp(-m_t))
        gain = per_chain(lambda b, h: gain_ref[:, h * dv:(h + 1) * dv])
        o_gate = _sigmoid(per_chain(lambda b, h: o_ref[b, rows, h * dv:(h + 1) * dv]))
        y = (hh * lax.rsqrt(jnp.mean(hh * hh, axis=-1, keepdims=True) + EPS) * gain * o_gate).astype(out_ref.dtype)
        for i, (b, h) in enumerate(chains):
            out_ref[b, rows, h * dv:(h + 1) * dv] = y[i]
        return carry

    lax.fori_loop(0, nchunks, chunk_body, 0)


def mlstm_block(z, gate_bias, head_gain, batch, seq, q_off, k_off, v_off, o_off, g_off):
    T, cols = z.shape
    H, L = MLSTM_HEADS, MLSTM_CHUNK
    width = head_gain.shape[0]
    dv = width // H
    dk = dv // 2
    ts = min(seq, 256)
    z3 = z.reshape(batch, seq, cols)

    def col(off, w):
        assert off % w == 0
        return pl.BlockSpec((batch, ts, w), lambda s: (0, s, off // w))

    n_state = batch * H
    out = pl.pallas_call(
        functools.partial(_mlstm_kernel, batch=batch, heads=H, dk=dk, dv=dv, chunk=L, nchunks=ts // L),
        grid=(seq // ts,),
        in_specs=[col(q_off, H * dk), col(k_off, H * dk), col(v_off, width), col(o_off, width), col(g_off, LANE),
                  pl.BlockSpec((1, LANE), lambda s: (0, 0)), pl.BlockSpec((1, width), lambda s: (0, 0))],
        out_specs=pl.BlockSpec((batch, ts, width), lambda s: (0, s, 0)),
        out_shape=jax.ShapeDtypeStruct((batch, seq, width), BF16),
        scratch_shapes=[pltpu.VMEM((n_state, dv + LANE, dk), F32),
                        pltpu.VMEM((n_state, SUBLANE, LANE), F32)],
        compiler_params=_params("arbitrary"),
        name="mlstm",
    )(z3, z3, z3, z3, z3, gate_bias, head_gain.reshape(1, width).astype(F32))
    return out.reshape(T, width)


def _pool_kernel(p_ref, w_ref, s_ref, out_ref, halo_ref, *, group, ts):
    si = pl.program_id(1)

    @pl.when(si == 0)
    def _():
        halo_ref[...] = jnp.zeros_like(halo_ref)

    p = p_ref[...]
    buf = jnp.concatenate([halo_ref[...], p], axis=0)
    t_abs = si * ts + lax.broadcasted_iota(jnp.int32, (ts, 1), 0)
    win = buf
    span = 1
    for g, w in enumerate(POOL_WINDOWS):
        while span < w:
            win = win + pltpu.roll(win, span, 0)
            span *= 2
        sl = slice(g * group, (g + 1) * group)
        cnt = jnp.minimum(t_abs + 1, w).astype(F32)
        mixed = win[POOL_HALO:, sl] / cnt - p[:, sl]
        y = _dot(mixed.astype(BF16), w_ref[g]) * s_ref[:, sl]
        out_ref[:, sl] = y.astype(out_ref.dtype)
    halo_ref[...] = p[ts - POOL_HALO:, :]


def pool_block(z, w_pool, pool_scale, batch, seq, p_off):
    T = z.shape[0]
    ng, group, _ = w_pool.shape
    width = ng * group
    ts = min(seq, 512)
    nblk = seq // ts
    assert p_off % width == 0
    return pl.pallas_call(
        functools.partial(_pool_kernel, group=group, ts=ts),
        grid=(batch, nblk),
        in_specs=[pl.BlockSpec((ts, width), lambda b, s: (b * nblk + s, p_off // width)),
                  pl.BlockSpec((ng, group, group), lambda b, s: (0, 0, 0)),
                  pl.BlockSpec((1, width), lambda b, s: (0, 0))],
        out_specs=pl.BlockSpec((ts, width), lambda b, s: (b * nblk + s, 0)),
        out_shape=jax.ShapeDtypeStruct((T, width), BF16),
        scratch_shapes=[pltpu.VMEM((POOL_HALO, width), F32)],
        compiler_params=_params("parallel", "arbitrary"),
        name="pool",
    )(z, w_pool.astype(BF16), pool_scale.reshape(1, width).astype(F32))


def _gla_kernel(q_ref, k_ref, v_ref, r_ref, gl_ref, wg_ref, bg_ref, gain_ref, out_ref, st_ref,
                *, batch, heads, dk, dv, chunk, nchunks):
    L = chunk
    Lh = L // 2

    @pl.when(pl.program_id(0) == 0)
    def _():
        st_ref[...] = jnp.zeros_like(st_ref)

    tri_b = _tril(L, BF16)
    row_id = lax.broadcasted_iota(jnp.int32, (L, 1), 0)
    row_half = lax.broadcasted_iota(jnp.int32, (Lh, 1), 0)
    lane_id = lax.broadcasted_iota(jnp.int32, (Lh, L), 1)
    qscale = dk ** -0.5
    chains = [(b, h) for b in range(batch) for h in range(heads)]
    n_chain = len(chains)

    def per_chain(fn):
        return jnp.stack([fn(b, h) for b, h in chains], axis=0)

    def chunk_body(c, carry):
        r0 = pl.multiple_of(c * L, L)
        rows = pl.ds(r0, L)
        b_seq = []
        for b in range(batch):
            la = _log_sigmoid(_dot(gl_ref[b, rows, :].astype(BF16), wg_ref[...]) + bg_ref[...]) / GLA_TAU
            b_seq.append(_cumsum_rows(la, tri_b))
        bb = per_chain(lambda b, h: b_seq[b][:, h * dk:(h + 1) * dk]) * LOG2E
        q = per_chain(lambda b, h: q_ref[b, rows, h * dk:(h + 1) * dk]) * qscale
        k = per_chain(lambda b, h: k_ref[b, rows, h * dk:(h + 1) * dk])

        att_half = [jnp.zeros((n_chain, Lh, L), F32), jnp.zeros((n_chain, Lh, L), F32)]
        for s in range(L):
            hf = s // Lh
            h0 = hf * Lh
            rel = jnp.where(row_half >= s - h0, bb[:, h0:h0 + Lh, :] - bb[:, s:s + 1, :], NEG_INF)
            col = jnp.sum(q[:, h0:h0 + Lh, :] * k[:, s:s + 1, :] * jnp.exp2(rel), axis=-1, keepdims=True)
            att_half[hf] = jnp.where(lane_id == s, col, att_half[hf])
        anchor = bb[:, Lh - 1:Lh, :]
        qd = (q * jnp.exp2(jnp.where(row_id >= Lh, bb - anchor, NEG_INF))).astype(BF16)
        kd = (k * jnp.exp2(jnp.where(row_id < Lh, anchor - bb, NEG_INF))).astype(BF16)
        att = jnp.concatenate(att_half, axis=1) + jnp.stack([_dot(qd[i], kd[i], NT_DIMS) for i in range(n_chain)], axis=0)
        att = att.astype(BF16)

        b_last = bb[:, L - 1:L, :]
        q_in = (q * jnp.exp2(bb)).astype(BF16)
        k_out = (k * jnp.exp2(b_last - bb)).astype(BF16)
        vb = [v_ref[b, rows, h * dv:(h + 1) * dv].astype(BF16) for b, h in chains]
        o = jnp.stack([_dot(att[i], vb[i]) + _dot(q_in[i], st_ref[i].astype(BF16), NT_DIMS) for i in range(n_chain)], axis=0)
        st_ref[...] = jnp.exp2(b_last) * st_ref[...] + jnp.stack([_dot(vb[i], k_out[i], TN_DIMS) for i in range(n_chain)], axis=0)

        gain = per_chain(lambda b, h: gain_ref[:, h * dv:(h + 1) * dv])
        r = per_chain(lambda b, h: r_ref[b, rows, h * dv:(h + 1) * dv])
        y = o * lax.rsqrt(jnp.mean(o * o, axis=-1, keepdims=True) + EPS) * gain * (r * _sigmoid(r))
        y = y.astype(out_ref.dtype)
        for i, (b, h) in enumerate(chains):
            out_ref[b, rows, h * dv:(h + 1) * dv] = y[i]
        return carry

    lax.fori_loop(0, nchunks, chunk_body, 0)


def gla_block(z, w_gate, b_gate, head_gain, batch, seq, q_off, k_off, v_off, r_off, g_off):
    T, cols = z.shape
    H, L = GLA_HEADS, GLA_CHUNK
    dkt = w_gate.shape[1]
    dvt = head_gain.shape[0]
    dk, dv = dkt // H, dvt // H
    ts = min(seq, 128)
    z3 = z.reshape(batch, seq, cols)

    def col(off, w):
        assert off % w == 0
        return pl.BlockSpec((batch, ts, w), lambda s: (0, s, off // w))

    out = pl.pallas_call(
        functools.partial(_gla_kernel, batch=batch, heads=H, dk=dk, dv=dv, chunk=L, nchunks=ts // L),
        grid=(seq // ts,),
        in_specs=[col(q_off, dkt), col(k_off, dkt), col(v_off, dvt), col(r_off, dvt), col(g_off, LANE),
                  pl.BlockSpec((LANE, dkt), lambda s: (0, 0)),
                  pl.BlockSpec((1, dkt), lambda s: (0, 0)),
                  pl.BlockSpec((1, dvt), lambda s: (0, 0))],
        out_specs=pl.BlockSpec((batch, ts, dvt), lambda s: (0, s, 0)),
        out_shape=jax.ShapeDtypeStruct((batch, seq, dvt), BF16),
        scratch_shapes=[pltpu.VMEM((batch * H, dv, dk), F32)],
        compiler_params=_params("arbitrary"),
        name="gla",
    )(z3, z3, z3, z3, z3, w_gate, b_gate.reshape(1, dkt).astype(F32), head_gain.reshape(1, dvt).astype(F32))
    return out.reshape(T, dvt)


def _extract_topk(vals, k, with_rank):
    rank = jnp.full(vals.shape, float(k), F32) if with_rank else None
    tops = []
    for i in range(k):
        mx = jnp.max(vals, axis=0, keepdims=True)
        hit = vals == mx
        if with_rank:
            rank = jnp.where(hit, float(i), rank)
        vals = jnp.where(hit, NEG_INF, vals)
        tops.append(mx)
    return jnp.concatenate(tops, axis=0), rank


def _count_partners(s1, sv2, tau, k):
    def test(row):
        return s1 + row >= tau

    def pick(bits, lo, step):
        if not bits:
            return sv2[lo:lo + 1, :]
        return jnp.where(bits[0], pick(bits[1:], lo + step, step // 2), pick(bits[1:], lo, step // 2))

    bits = []
    step = k // 2
    while step >= 1:
        bits.append(test(pick(bits, step - 1, k // 2)))
        step //= 2
    cnt = jnp.zeros_like(s1)
    weight = k // 2
    for bit in bits:
        cnt = cnt + jnp.where(bit, float(weight), 0.0)
        weight //= 2
    return cnt + jnp.where(test(sv2[k - 1:k, :]), 1.0, 0.0)


def _dup_bf16(x):
    hi = pltpu.bitcast(x.astype(BF16).astype(F32), jnp.uint32)
    return hi | lax.shift_right_logical(hi, jnp.uint32(16))


def _peer_topk_kernel(qt_ref, keys_ref, cnt1_ref, p1_ref, r2_ref, p2_ref, *, heads, half, topk):
    K = topk
    for h in range(heads):
        s1 = _dot(keys_ref[h, 0], qt_ref[(2 * h) * half:(2 * h + 1) * half, :])
        s2 = _dot(keys_ref[h, 1], qt_ref[(2 * h + 1) * half:(2 * h + 2) * half, :])
        sv1, _ = _extract_topk(s1, K, False)
        sv2, rank2 = _extract_topk(s2, K, True)
        cand = []
        a = 0
        while K // (a + 1) > 1:
            nb = K // (a + 1)
            rows = -(-nb // SUBLANE) * SUBLANE
            c = sv1[a:a + 1, :] + sv2[:rows, :]
            if nb < rows:
                c = jnp.where(lax.broadcasted_iota(jnp.int32, (rows, 1), 0) < nb, c, NEG_INF)
            cand.append(c)
            a += 1
        n_single = a
        cand.append(sv1[n_single:, :] + sv2[0:1, :])
        work = jnp.concatenate(cand, axis=0)
        tau = None
        for _ in range(K):
            tau = jnp.max(work, axis=0, keepdims=True)
            work = jnp.where(work == tau, NEG_INF, work)
        cmax = sv1[0:1, :] + sv2[0:1, :]
        zsum = jnp.zeros_like(cmax)
        for c in cand:
            zsum = zsum + jnp.sum(jnp.where(c >= tau, jnp.exp(c - cmax), 0.0), axis=0, keepdims=True)
        cnt1_ref[h] = _dup_bf16(_count_partners(s1, sv2, tau, K))
        p1_ref[h] = _dup_bf16(jnp.exp(s1 - sv1[0:1, :]) / zsum)
        r2_ref[h] = pltpu.bitcast(rank2.astype(BF16), jnp.uint32)
        p2_ref[h] = pltpu.bitcast(jnp.exp(s2 - sv2[0:1, :]).astype(BF16), jnp.uint32)


def peer_topk(qt, keys):
    heads, _, n_keys, half = keys.shape
    T = qt.shape[1]
    tt = min(T, 256)
    out_f = jax.ShapeDtypeStruct((heads, n_keys, T), jnp.uint32)
    out_b = jax.ShapeDtypeStruct((heads, n_keys // 2, T), jnp.uint32)
    ospec = pl.BlockSpec((heads, n_keys, tt), lambda i: (0, 0, i))
    ospec_b = pl.BlockSpec((heads, n_keys // 2, tt), lambda i: (0, 0, i))
    return pl.pallas_call(
        functools.partial(_peer_topk_kernel, heads=heads, half=half, topk=PEER_TOPK),
        grid=(T // tt,),
        in_specs=[pl.BlockSpec((heads * 2 * half, tt), lambda i: (0, i)),
                  pl.BlockSpec((heads, 2, n_keys, half), lambda i: (0, 0, 0, 0))],
        out_specs=[ospec, ospec, ospec_b, ospec_b],
        out_shape=[out_f, out_f, out_b, out_b],
        compiler_params=_params("parallel"),
        name="peer_topk",
    )(qt, keys)


def _peer_gate_tile(a, cnt_rows, p1_rows, r2_ref, p2_ref, lanes, *, heads, n_keys):
    ab = a.astype(BF16)
    gelu = 0.5 * ab * (1.0 + lax.erf(ab * (2.0 ** -0.5)))
    gate = jnp.zeros((n_keys, LANE), BF16)
    for h in range(heads):
        cnt = pltpu.bitcast(jnp.broadcast_to(cnt_rows[h][:, lanes], (n_keys // 2, LANE)), BF16)
        p1 = pltpu.bitcast(jnp.broadcast_to(p1_rows[h][:, lanes], (n_keys // 2, LANE)), BF16)
        r2 = pltpu.bitcast(r2_ref[h, :, lanes], BF16)
        p2 = pltpu.bitcast(p2_ref[h, :, lanes], BF16)
        gate = gate + jnp.where(r2 < cnt, p2 * p1, jnp.zeros_like(gate))
    return (gate * gelu).T


def _peer_dense_step(blk, c_new, c_old, xt_ref, u_ref, v_ref, cnt1_ref, p1_ref, r2_ref, p2_ref, out_ref,
                     *, heads, n_keys, groups):
    tm = xt_ref.shape[1]
    dc = out_ref.shape[1] // groups
    for r in range(groups):
        if c_new is not None:
            e1 = blk * groups + r
            rows = slice(r * n_keys, (r + 1) * n_keys)
            a = _dot(u_ref[rows, :], xt_ref[...])
            cnt_rows = [cnt1_ref[h, pl.ds(e1, 1), :] for h in range(heads)]
            p1_rows = [p1_ref[h, pl.ds(e1, 1), :] for h in range(heads)]
            for lt in range(tm // LANE):
                lanes = slice(lt * LANE, (lt + 1) * LANE)
                c_new[lanes, rows] = _peer_gate_tile(a[:, lanes], cnt_rows, p1_rows, r2_ref, p2_ref, lanes,
                                                     heads=heads, n_keys=n_keys)
        if c_old is not None:
            cols = slice(r * dc, (r + 1) * dc)
            out_ref[:, cols] += _dot(c_old[...], v_ref[:, cols])


def _peer_dense_kernel(xt_ref, u_ref, v_ref, cnt1_ref, p1_ref, r2_ref, p2_ref, res_ref, gain_ref, *rest,
                       heads, n_keys, groups, nblk, final):
    if final:
        out_ref, c0_ref, c1_ref = rest
    else:
        out_ref, norm_ref, c0_ref, c1_ref = rest
    g = pl.program_id(1)
    step = functools.partial(_peer_dense_step, g, xt_ref=xt_ref, u_ref=u_ref, v_ref=v_ref,
                             cnt1_ref=cnt1_ref, p1_ref=p1_ref, r2_ref=r2_ref, p2_ref=p2_ref, out_ref=out_ref,
                             heads=heads, n_keys=n_keys, groups=groups)
    bufs = (c0_ref, c1_ref)
    inner = jnp.logical_and(g > 0, g < nblk)

    @pl.when(g == 0)
    def _():
        out_ref[...] = res_ref[...]
        step(bufs[0], None)

    @pl.when(jnp.logical_and(inner, lax.rem(g, 2) == 0))
    def _():
        step(bufs[0], bufs[1])

    @pl.when(jnp.logical_and(inner, lax.rem(g, 2) == 1))
    def _():
        step(bufs[1], bufs[0])

    @pl.when(g == nblk)
    def _():
        step(None, bufs[(nblk - 1) % 2])
        x = out_ref[...]
        y = x * lax.rsqrt(jnp.mean(x * x, axis=-1, keepdims=True) + EPS) * gain_ref[...]
        if final:
            out_ref[...] = y
        else:
            norm_ref[...] = y.astype(norm_ref.dtype)


def peer_dense(xt, u, v, cnt1, p1, r2, p2, residual, post_gain, final):
    D, T = xt.shape
    heads, n_keys, _ = cnt1.shape
    E = u.shape[0]
    groups = 8
    eb = groups * n_keys
    nblk = E // eb
    tm = min(T, 512)
    gspec = pl.BlockSpec((heads, n_keys, tm), lambda i, g: (0, 0, i))
    gspec_b = pl.BlockSpec((heads, n_keys // 2, tm), lambda i, g: (0, 0, i))
    row_spec = pl.BlockSpec((tm, D), lambda i, g: (i, 0))
    out_specs = row_spec if final else [row_spec, row_spec]
    out_shape = jax.ShapeDtypeStruct((T, D), F32)
    if not final:
        out_shape = [out_shape, jax.ShapeDtypeStruct((T, D), BF16)]
    return pl.pallas_call(
        functools.partial(_peer_dense_kernel, heads=heads, n_keys=n_keys, groups=groups, nblk=nblk, final=final),
        grid=(T // tm, nblk + 1),
        in_specs=[pl.BlockSpec((D, tm), lambda i, g: (0, i)),
                  pl.BlockSpec((eb, D), lambda i, g: (jnp.minimum(g, nblk - 1), 0)),
                  pl.BlockSpec((eb, D), lambda i, g: (jnp.maximum(g - 1, 0), 0)),
                  gspec, gspec, gspec_b, gspec_b,
                  pl.BlockSpec((tm, D), lambda i, g: (i, 0), pipeline_mode=pl.Buffered(1)),
                  pl.BlockSpec((1, D), lambda i, g: (0, 0))],
        out_specs=out_specs,
        out_shape=out_shape,
        scratch_shapes=[pltpu.VMEM((tm, eb), BF16), pltpu.VMEM((tm, eb), BF16)],
        compiler_params=_params("parallel", "arbitrary"),
        name="peer_dense",
    )(xt, u, v, cnt1, p1, r2, p2, residual, post_gain.reshape(1, D).astype(F32))


def _cast_kernel(x_ref, o_ref):
    o_ref[...] = x_ref[0].astype(o_ref.dtype)


def cast_table(tabs, layer, dtype):
    _, rows, cols = tabs.shape
    tr = min(rows, 1024)
    return pl.pallas_call(
        _cast_kernel,
        grid=(rows // tr,),
        in_specs=[pl.BlockSpec((1, tr, cols), lambda i: (layer, i, 0))],
        out_specs=pl.BlockSpec((tr, cols), lambda i: (i, 0)),
        out_shape=jax.ShapeDtypeStruct((rows, cols), dtype),
        compiler_params=_params("parallel"),
        name="cast_table",
    )(tabs)


def peer_layer(x, ht, w_q, sub_keys, u_tabs, v_tabs, layer, post_gain, final):
    qt = matmul(w_q.T.astype(BF16), ht, BF16, tn=512, name="peer_q")
    cnt1, p1, r2, p2 = peer_topk(qt, sub_keys.astype(BF16))
    return peer_dense(ht, cast_table(u_tabs, layer, BF16), cast_table(v_tabs, layer, BF16), cnt1, p1, r2, p2, x,
                      post_gain, final)


def _pad_cols(w, n):
    return jnp.pad(w, ((0, 0), (0, n - w.shape[1])))


def even_layer(x, h, batch, seq, w_in, b_igate, b_fgate, head_gain, w_pool, pool_scale, w_out, peer_gain):
    H = MLSTM_HEADS
    width = head_gain.shape[0]
    qk = width // 2
    pool_w = pool_scale.shape[0]
    n_main = 2 * qk + 2 * width
    w_cat = jnp.concatenate([w_in[:, :n_main], w_in[:, n_main + 2 * H:], w_in[:, n_main:n_main + 2 * H]], axis=1)
    tn = 3 * MXU_COLS
    z = matmul(h, _pad_cols(w_cat, -(-w_cat.shape[1] // tn) * tn).astype(BF16), F32, tn=tn, name="even_in")
    gate_bias = jnp.pad(jnp.concatenate([b_igate, b_fgate]), (0, LANE - 2 * H)).reshape(1, LANE).astype(F32)
    hm = mlstm_block(z, gate_bias, head_gain, batch, seq, 0, qk, 2 * qk, 2 * qk + width, n_main + pool_w)
    hp = pool_block(z, w_pool, pool_scale, batch, seq, n_main)
    return out_proj([hm, hp], w_out.astype(BF16), x, peer_gain)


def odd_layer(x, h, batch, seq, w_in, w_gate, b_gate, head_gain, w_out, peer_gain):
    rank, dkt = w_gate.shape
    dvt = head_gain.shape[0]
    n_main = 2 * dkt + 2 * dvt
    tn = 5 * MXU_COLS
    z = matmul(h, _pad_cols(w_in, -(-w_in.shape[1] // tn) * tn).astype(BF16), F32, tn=tn, name="odd_in")
    wg = jnp.pad(w_gate, ((0, LANE - rank), (0, 0))).astype(BF16)
    y = gla_block(z, wg, b_gate, head_gain, batch, seq, 0, dkt, 2 * dkt, 2 * dkt + dvt, n_main)
    return out_proj([y], w_out.astype(BF16), x, peer_gain)


def kernel(x, e_norm, e_w_in, e_b_igate, e_b_fgate, e_head_gain, e_w_pool, e_pool_scale, e_w_out, o_norm, o_w_in, o_w_gate, o_b_gate, o_head_gain, o_w_out, f_norm, f_w_q, f_sub_keys, f_u, f_v, final_norm):
    B, S, D = x.shape
    depth = f_norm.shape[0]

    def mixer_gain(layer):
        return e_norm[layer // 2] if layer % 2 == 0 else o_norm[layer // 2]

    xs = x.reshape(B * S, D)
    h = rmsnorm(xs, mixer_gain(0), BF16)
    for layer in range(depth):
        j = layer // 2
        if layer % 2 == 0:
            xs, ht = even_layer(xs, h, B, S, e_w_in[j], e_b_igate[j], e_b_fgate[j], e_head_gain[j],
                                e_w_pool[j], e_pool_scale[j], e_w_out[j], f_norm[layer])
        else:
            xs, ht = odd_layer(xs, h, B, S, o_w_in[j], o_w_gate[j], o_b_gate[j], o_head_gain[j], o_w_out[j],
                               f_norm[layer])
        if layer == depth - 1:
            return peer_layer(xs, ht, f_w_q[layer], f_sub_keys[layer], f_u, f_v, layer,
                              final_norm, True).reshape(B, S, D)
        xs, h = peer_layer(xs, ht, f_w_q[layer], f_sub_keys[layer], f_u, f_v, layer,
                           mixer_gain(layer + 1), False)
```

```python
import functools

import jax
import jax.numpy as jnp
from jax import lax
from jax.experimental import pallas as pl
from jax.experimental.pallas import tpu as pltpu

F32 = jnp.float32
BF16 = jnp.bfloat16
EPS = 1e-6
LANE = 128
SUBLANE = 8
MXU_COLS = 256
VMEM_LIMIT = 56 * 1024 * 1024

MLSTM_HEADS = 4
MLSTM_CHUNK = 64
POOL_WINDOWS = (2, 4, 8, 16)
POOL_HALO = 16
GLA_HEADS = 4
GLA_CHUNK = 32
GLA_TAU = 16.0
PEER_TOPK = 16
NEG_INF = float("-inf")
LOG2E = 1.4426950408889634

NT_DIMS = (((1,), (1,)), ((), ()))
TN_DIMS = (((0,), (0,)), ((), ()))


def _params(*sem):
    return pltpu.CompilerParams(dimension_semantics=sem, vmem_limit_bytes=VMEM_LIMIT)


def _log_sigmoid(x):
    return jnp.minimum(x, 0.0) - jnp.log1p(jnp.exp(-jnp.abs(x)))


def _sigmoid(x):
    return 1.0 / (1.0 + jnp.exp(-x))


def _dot(a, b, dims=None):
    if dims is None:
        return jnp.dot(a, b, preferred_element_type=F32)
    return lax.dot_general(a, b, dims, preferred_element_type=F32)


def _cumsum_rows(x, tri):
    hi = x.astype(BF16)
    lo = (x - hi.astype(F32)).astype(BF16)
    return _dot(tri, hi) + _dot(tri, lo)


def _tril(n, dtype):
    r = lax.broadcasted_iota(jnp.int32, (n, n), 0)
    c = lax.broadcasted_iota(jnp.int32, (n, n), 1)
    return (c <= r).astype(dtype)


def _rms_kernel(x_ref, g_ref, o_ref):
    x = x_ref[...]
    y = x * lax.rsqrt(jnp.mean(x * x, axis=-1, keepdims=True) + EPS) * g_ref[...]
    o_ref[...] = y.astype(o_ref.dtype)


def rmsnorm(x, g, out_dtype):
    T, D = x.shape
    tm = min(T, 512)
    return pl.pallas_call(
        _rms_kernel,
        grid=(T // tm,),
        in_specs=[pl.BlockSpec((tm, D), lambda i: (i, 0)), pl.BlockSpec((1, D), lambda i: (0, 0))],
        out_specs=pl.BlockSpec((tm, D), lambda i: (i, 0)),
        out_shape=jax.ShapeDtypeStruct((T, D), out_dtype),
        compiler_params=_params("parallel"),
        name="rmsnorm",
    )(x, g.reshape(1, D).astype(F32))


def _mm_kernel(a_ref, b_ref, o_ref):
    o_ref[...] = _dot(a_ref[...], b_ref[...]).astype(o_ref.dtype)


def matmul(a, b, out_dtype, tn, name="matmul"):
    M, K = a.shape
    N = b.shape[1]
    tm = min(M, 1024)
    tn = min(N, tn)
    return pl.pallas_call(
        _mm_kernel,
        grid=(M // tm, N // tn),
        in_specs=[pl.BlockSpec((tm, K), lambda i, j: (i, 0)), pl.BlockSpec((K, tn), lambda i, j: (0, j))],
        out_specs=pl.BlockSpec((tm, tn), lambda i, j: (i, j)),
        out_shape=jax.ShapeDtypeStruct((M, N), out_dtype),
        compiler_params=_params("parallel", "parallel"),
        name=name,
    )(a, b)


def _out_proj_kernel(*refs, n_parts):
    parts = refs[:n_parts]
    w_ref, res_ref, gain_ref, x_ref, nt_ref = refs[n_parts:]
    acc = res_ref[...]
    k0 = 0
    for p in parts:
        kp = p.shape[1]
        acc = acc + _dot(p[...], w_ref[k0:k0 + kp, :])
        k0 += kp
    x_ref[...] = acc
    y = acc * lax.rsqrt(jnp.mean(acc * acc, axis=-1, keepdims=True) + EPS) * gain_ref[...]
    nt_ref[...] = y.T.astype(nt_ref.dtype)


def out_proj(parts, w, residual, norm_gain):
    T, N = residual.shape
    K = w.shape[0]
    tm = min(T, 512)
    in_specs = [pl.BlockSpec((tm, p.shape[1]), lambda i: (i, 0)) for p in parts]
    in_specs += [pl.BlockSpec((K, N), lambda i: (0, 0), pipeline_mode=pl.Buffered(1)),
                 pl.BlockSpec((tm, N), lambda i: (i, 0)),
                 pl.BlockSpec((1, N), lambda i: (0, 0))]
    return pl.pallas_call(
        functools.partial(_out_proj_kernel, n_parts=len(parts)),
        grid=(T // tm,),
        in_specs=in_specs,
        out_specs=[pl.BlockSpec((tm, N), lambda i: (i, 0)), pl.BlockSpec((N, tm), lambda i: (0, i))],
        out_shape=[jax.ShapeDtypeStruct((T, N), F32), jax.ShapeDtypeStruct((N, T), BF16)],
        compiler_params=_params("parallel"),
        name="out_proj",
    )(*parts, w, residual, norm_gain.reshape(1, N).astype(F32))


def _mlstm_kernel(q_ref, k_ref, v_ref, o_ref, g_ref, gb_ref, gain_ref, out_ref, cn_ref, m_ref,
                  *, batch, heads, dk, dv, chunk, nchunks):
    L = chunk

    @pl.when(pl.program_id(0) == 0)
    def _():
        cn_ref[...] = jnp.zeros_like(cn_ref)
        m_ref[...] = jnp.zeros_like(m_ref)

    tri_b = _tril(L, BF16)
    causal = _tril(L, jnp.int32) > 0
    ones_col = (lax.broadcasted_iota(jnp.int32, (L, LANE), 1) == 0).astype(F32)
    kscale = dk ** -0.5

    chains = [(b, h) for b in range(batch) for h in range(heads)]
    n_chain = len(chains)

    def per_chain(fn):
        return jnp.stack([fn(b, h) for b, h in chains], axis=0)

    def chunk_body(c, carry):
        r0 = pl.multiple_of(c * L, L)
        rows = pl.ds(r0, L)
        g = [g_ref[b, rows, :] + gb_ref[...] for b in range(batch)]
        bc = [_cumsum_rows(_log_sigmoid(g[b]), tri_b) for b in range(batch)]
        g_t = [x.T for x in g]
        bc_t = [x.T for x in bc]
        i_col = per_chain(lambda b, h: g[b][:, h:h + 1])
        i_row = per_chain(lambda b, h: g_t[b][h:h + 1, :])
        b_col = per_chain(lambda b, h: bc[b][:, heads + h:heads + h + 1])
        b_row = per_chain(lambda b, h: bc_t[b][heads + h:heads + h + 1, :])
        m_prev = m_ref[:, 0:1, 0:1]

        dmat = jnp.where(causal, b_col - b_row + i_row, NEG_INF)
        inter = b_col + m_prev
        m_t = jnp.maximum(jnp.max(dmat, axis=-1, keepdims=True), inter)
        a_inter = jnp.exp(inter - m_t)
        decay = jnp.exp(dmat - m_t)
        b_last = b_col[:, L - 1:L, :]
        dec = b_last - b_col + i_col
        m_new = jnp.maximum(b_last + m_prev, jnp.max(dec, axis=1, keepdims=True))
        wk = jnp.exp(dec - m_new)
        sc = jnp.exp(b_last + m_prev - m_new)

        qb = [q_ref[b, rows, h * dk:(h + 1) * dk].astype(BF16) for b, h in chains]
        kb = [(k_ref[b, rows, h * dk:(h + 1) * dk] * kscale).astype(BF16) for b, h in chains]
        v_aug = per_chain(lambda b, h: jnp.concatenate([v_ref[b, rows, h * dv:(h + 1) * dv], ones_col], axis=1))
        v_aug_b = v_aug.astype(BF16)
        s = (jnp.stack([_dot(qb[i], kb[i], NT_DIMS) for i in range(n_chain)], axis=0) * decay).astype(BF16)
        intra = jnp.stack([_dot(s[i], v_aug_b[i]) for i in range(n_chain)], axis=0)
        inter_out = jnp.stack([_dot(qb[i], cn_ref[i].astype(BF16), NT_DIMS) for i in range(n_chain)], axis=0)
        hn = intra + a_inter * inter_out

        wv = (wk * v_aug).astype(BF16)
        cn_ref[...] = sc * cn_ref[...] + jnp.stack([_dot(wv[i], kb[i], TN_DIMS) for i in range(n_chain)], axis=0)
        m_ref[...] = jnp.broadcast_to(m_new, m_ref.shape)

        hh = hn[:, :, :dv] / jnp.maximum(jnp.abs(hn[:, :, dv:dv + 1]), jnp.exp(-m_t))
        gain = per_chain(lambda b, h: gain_ref[:, h * dv:(h + 1) * dv])
        o_gate = _sigmoid(per_chain(lambda b, h: o_ref[b, rows, h * dv:(h + 1) * dv]))
        y = (hh * lax.rsqrt(jnp.mean(hh * hh, axis=-1, keepdims=True) + EPS) * gain * o_gate).astype(out_ref.dtype)
        for i, (b, h) in enumerate(chains):
            out_ref[b, rows, h * dv:(h + 1) * dv] = y[i]
        return carry

    lax.fori_loop(0, nchunks, chunk_body, 0)


def mlstm_block(z, gate_bias, head_gain, batch, seq, q_off, k_off, v_off, o_off, g_off):
    T, cols = z.shape
    H, L = MLSTM_HEADS, MLSTM_CHUNK
    width = head_gain.shape[0]
    dv = width // H
    dk = dv // 2
    ts = min(seq, 256)
    z3 = z.reshape(batch, seq, cols)

    def col(off, w):
        assert off % w == 0
        return pl.BlockSpec((batch, ts, w), lambda s: (0, s, off // w))

    n_state = batch * H
    out = pl.pallas_call(
        functools.partial(_mlstm_kernel, batch=batch, heads=H, dk=dk, dv=dv, chunk=L, nchunks=ts // L),
        grid=(seq // ts,),
        in_specs=[col(q_off, H * dk), col(k_off, H * dk), col(v_off, width), col(o_off, width), col(g_off, LANE),
                  pl.BlockSpec((1, LANE), lambda s: (0, 0)), pl.BlockSpec((1, width), lambda s: (0, 0))],
        out_specs=pl.BlockSpec((batch, ts, width), lambda s: (0, s, 0)),
        out_shape=jax.ShapeDtypeStruct((batch, seq, width), BF16),
        scratch_shapes=[pltpu.VMEM((n_state, dv + LANE, dk), F32),
                        pltpu.VMEM((n_state, SUBLANE, LANE), F32)],
        compiler_params=_params("arbitrary"),
        name="mlstm",
    )(z3, z3, z3, z3, z3, gate_bias, head_gain.reshape(1, width).astype(F32))
    return out.reshape(T, width)


def _pool_kernel(p_ref, w_ref, s_ref, out_ref, halo_ref, *, group, ts):
    si = pl.program_id(1)

    @pl.when(si == 0)
    def _():
        halo_ref[...] = jnp.zeros_like(halo_ref)

    p = p_ref[...]
    buf = jnp.concatenate([halo_ref[...], p], axis=0)
    t_abs = si * ts + lax.broadcasted_iota(jnp.int32, (ts, 1), 0)
    win = buf
    span = 1
    for g, w in enumerate(POOL_WINDOWS):
        while span < w:
            win = win + pltpu.roll(win, span, 0)
            span *= 2
        sl = slice(g * group, (g + 1) * group)
        cnt = jnp.minimum(t_abs + 1, w).astype(F32)
        mixed = win[POOL_HALO:, sl] / cnt - p[:, sl]
        y = _dot(mixed.astype(BF16), w_ref[g]) * s_ref[:, sl]
        out_ref[:, sl] = y.astype(out_ref.dtype)
    halo_ref[...] = p[ts - POOL_HALO:, :]


def pool_block(z, w_pool, pool_scale, batch, seq, p_off):
    T = z.shape[0]
    ng, group, _ = w_pool.shape
    width = ng * group
    ts = min(seq, 512)
    nblk = seq // ts
    assert p_off % width == 0
    return pl.pallas_call(
        functools.partial(_pool_kernel, group=group, ts=ts),
        grid=(batch, nblk),
        in_specs=[pl.BlockSpec((ts, width), lambda b, s: (b * nblk + s, p_off // width)),
                  pl.BlockSpec((ng, group, group), lambda b, s: (0, 0, 0)),
                  pl.BlockSpec((1, width), lambda b, s: (0, 0))],
        out_specs=pl.BlockSpec((ts, width), lambda b, s: (b * nblk + s, 0)),
        out_shape=jax.ShapeDtypeStruct((T, width), BF16),
        scratch_shapes=[pltpu.VMEM((POOL_HALO, width), F32)],
        compiler_params=_params("parallel", "arbitrary"),
        name="pool",
    )(z, w_pool.astype(BF16), pool_scale.reshape(1, width).astype(F32))


def _gla_kernel(q_ref, k_ref, v_ref, r_ref, gl_ref, wg_ref, bg_ref, gain_ref, out_ref, st_ref,
                *, batch, heads, dk, dv, chunk, nchunks):
    L = chunk
    SB = SUBLANE

    @pl.when(pl.program_id(0) == 0)
    def _():
        st_ref[...] = jnp.zeros_like(st_ref)

    tri_b = _tril(L, BF16)
    row_id = lax.broadcasted_iota(jnp.int32, (L, 1), 0)
    row_sub = lax.broadcasted_iota(jnp.int32, (SB, 1), 0)
    lane_id = lax.broadcasted_iota(jnp.int32, (SB, L), 1)
    qscale = dk ** -0.5
    chains = [(b, h) for b in range(batch) for h in range(heads)]
    n_chain = len(chains)

    def per_chain(fn):
        return jnp.stack([fn(b, h) for b, h in chains], axis=0)

    def chunk_body(c, carry):
        r0 = pl.multiple_of(c * L, L)
        rows = pl.ds(r0, L)
        b_seq = []
        for b in range(batch):
            la = _log_sigmoid(_dot(gl_ref[b, rows, :].astype(BF16), wg_ref[...]) + bg_ref[...]) / GLA_TAU
            b_seq.append(_cumsum_rows(la, tri_b))
        bb = per_chain(lambda b, h: b_seq[b][:, h * dk:(h + 1) * dk]) * LOG2E
        q = per_chain(lambda b, h: q_ref[b, rows, h * dk:(h + 1) * dk]) * qscale
        k = per_chain(lambda b, h: k_ref[b, rows, h * dk:(h + 1) * dk])

        att_sub = [jnp.zeros((n_chain, SB, L), F32) for _ in range(L // SB)]
        for s in range(L):
            blk = s // SB
            h0 = blk * SB
            rel = jnp.where(row_sub >= s - h0, bb[:, h0:h0 + SB, :] - bb[:, s:s + 1, :], NEG_INF)
            col = jnp.sum(q[:, h0:h0 + SB, :] * k[:, s:s + 1, :] * jnp.exp2(rel), axis=-1, keepdims=True)
            att_sub[blk] = jnp.where(lane_id == s, col, att_sub[blk])
        att = jnp.concatenate(att_sub, axis=1)
        for j in range(L // SB - 1):
            end = (j + 1) * SB
            anchor = bb[:, end - 1:end, :]
            qd = (q * jnp.exp2(jnp.where(row_id >= end, bb - anchor, NEG_INF))).astype(BF16)
            in_blk = jnp.logical_and(row_id >= end - SB, row_id < end)
            kd = (k * jnp.exp2(jnp.where(in_blk, anchor - bb, NEG_INF))).astype(BF16)
            att = att + jnp.stack([_dot(qd[i], kd[i], NT_DIMS) for i in range(n_chain)], axis=0)
        att = att.astype(BF16)

        b_last = bb[:, L - 1:L, :]
        q_in = (q * jnp.exp2(bb)).astype(BF16)
        k_out = (k * jnp.exp2(b_last - bb)).astype(BF16)
        vb = [v_ref[b, rows, h * dv:(h + 1) * dv].astype(BF16) for b, h in chains]
        o = jnp.stack([_dot(att[i], vb[i]) + _dot(q_in[i], st_ref[i].astype(BF16), NT_DIMS) for i in range(n_chain)], axis=0)
        st_ref[...] = jnp.exp2(b_last) * st_ref[...] + jnp.stack([_dot(vb[i], k_out[i], TN_DIMS) for i in range(n_chain)], axis=0)

        gain = per_chain(lambda b, h: gain_ref[:, h * dv:(h + 1) * dv])
        r = per_chain(lambda b, h: r_ref[b, rows, h * dv:(h + 1) * dv])
        y = o * lax.rsqrt(jnp.mean(o * o, axis=-1, keepdims=True) + EPS) * gain * (r * _sigmoid(r))
        y = y.astype(out_ref.dtype)
        for i, (b, h) in enumerate(chains):
            out_ref[b, rows, h * dv:(h + 1) * dv] = y[i]
        return carry

    lax.fori_loop(0, nchunks, chunk_body, 0)


def gla_block(z, w_gate, b_gate, head_gain, batch, seq, q_off, k_off, v_off, r_off, g_off):
    T, cols = z.shape
    H, L = GLA_HEADS, GLA_CHUNK
    dkt = w_gate.shape[1]
    dvt = head_gain.shape[0]
    dk, dv = dkt // H, dvt // H
    ts = min(seq, 128)
    z3 = z.reshape(batch, seq, cols)

    def col(off, w):
        assert off % w == 0
        return pl.BlockSpec((batch, ts, w), lambda s: (0, s, off // w))

    out = pl.pallas_call(
        functools.partial(_gla_kernel, batch=batch, heads=H, dk=dk, dv=dv, chunk=L, nchunks=ts // L),
        grid=(seq // ts,),
        in_specs=[col(q_off, dkt), col(k_off, dkt), col(v_off, dvt), col(r_off, dvt), col(g_off, LANE),
                  pl.BlockSpec((LANE, dkt), lambda s: (0, 0)),
                  pl.BlockSpec((1, dkt), lambda s: (0, 0)),
                  pl.BlockSpec((1, dvt), lambda s: (0, 0))],
        out_specs=pl.BlockSpec((batch, ts, dvt), lambda s: (0, s, 0)),
        out_shape=jax.ShapeDtypeStruct((batch, seq, dvt), BF16),
        scratch_shapes=[pltpu.VMEM((batch * H, dv, dk), F32)],
        compiler_params=_params("arbitrary"),
        name="gla",
    )(z3, z3, z3, z3, z3, w_gate, b_gate.reshape(1, dkt).astype(F32), head_gain.reshape(1, dvt).astype(F32))
    return out.reshape(T, dvt)


def _extract_topk(vals, k, with_rank):
    rank = jnp.full(vals.shape, float(k), F32) if with_rank else None
    tops = []
    for i in range(k):
        mx = jnp.max(vals, axis=0, keepdims=True)
        hit = vals == mx
        if with_rank:
            rank = jnp.where(hit, float(i), rank)
        vals = jnp.where(hit, NEG_INF, vals)
        tops.append(mx)
    return jnp.concatenate(tops, axis=0), rank


def _count_partners(s1, sv2, tau, k):
    def test(row):
        return s1 + row >= tau

    def pick(bits, lo, step):
        if not bits:
            return sv2[lo:lo + 1, :]
        return jnp.where(bits[0], pick(bits[1:], lo + step, step // 2), pick(bits[1:], lo, step // 2))

    bits = []
    step = k // 2
    while step >= 1:
        bits.append(test(pick(bits, step - 1, k // 2)))
        step //= 2
    cnt = jnp.zeros_like(s1)
    weight = k // 2
    for bit in bits:
        cnt = cnt + jnp.where(bit, float(weight), 0.0)
        weight //= 2
    return cnt + jnp.where(test(sv2[k - 1:k, :]), 1.0, 0.0)


def _dup_bf16(x):
    hi = pltpu.bitcast(x.astype(BF16).astype(F32), jnp.uint32)
    return hi | lax.shift_right_logical(hi, jnp.uint32(16))


def _peer_topk_kernel(qt_ref, keys_ref, cnt1_ref, p1_ref, r2_ref, p2_ref, *, heads, half, topk):
    K = topk
    for h in range(heads):
        s1 = _dot(keys_ref[h, 0], qt_ref[(2 * h) * half:(2 * h + 1) * half, :])
        s2 = _dot(keys_ref[h, 1], qt_ref[(2 * h + 1) * half:(2 * h + 2) * half, :])
        sv1, _ = _extract_topk(s1, K, False)
        sv2, rank2 = _extract_topk(s2, K, True)
        cand = []
        a = 0
        while K // (a + 1) > 1:
            nb = K // (a + 1)
            rows = -(-nb // SUBLANE) * SUBLANE
            c = sv1[a:a + 1, :] + sv2[:rows, :]
            if nb < rows:
                c = jnp.where(lax.broadcasted_iota(jnp.int32, (rows, 1), 0) < nb, c, NEG_INF)
            cand.append(c)
            a += 1
        n_single = a
        cand.append(sv1[n_single:, :] + sv2[0:1, :])
        work = jnp.concatenate(cand, axis=0)
        tau = None
        for _ in range(K):
            tau = jnp.max(work, axis=0, keepdims=True)
            work = jnp.where(work == tau, NEG_INF, work)
        cmax = sv1[0:1, :] + sv2[0:1, :]
        zsum = jnp.zeros_like(cmax)
        for c in cand:
            zsum = zsum + jnp.sum(jnp.where(c >= tau, jnp.exp(c - cmax), 0.0), axis=0, keepdims=True)
        cnt1_ref[h] = _dup_bf16(_count_partners(s1, sv2, tau, K))
        p1_ref[h] = _dup_bf16(jnp.exp(s1 - sv1[0:1, :]) / zsum)
        r2_ref[h] = pltpu.bitcast(rank2.astype(BF16), jnp.uint32)
        p2_ref[h] = pltpu.bitcast(jnp.exp(s2 - sv2[0:1, :]).astype(BF16), jnp.uint32)


def peer_topk(qt, keys):
    heads, _, n_keys, half = keys.shape
    T = qt.shape[1]
    tt = min(T, 256)
    out_f = jax.ShapeDtypeStruct((heads, n_keys, T), jnp.uint32)
    out_b = jax.ShapeDtypeStruct((heads, n_keys // 2, T), jnp.uint32)
    ospec = pl.BlockSpec((heads, n_keys, tt), lambda i: (0, 0, i))
    ospec_b = pl.BlockSpec((heads, n_keys // 2, tt), lambda i: (0, 0, i))
    return pl.pallas_call(
        functools.partial(_peer_topk_kernel, heads=heads, half=half, topk=PEER_TOPK),
        grid=(T // tt,),
        in_specs=[pl.BlockSpec((heads * 2 * half, tt), lambda i: (0, i)),
                  pl.BlockSpec((heads, 2, n_keys, half), lambda i: (0, 0, 0, 0))],
        out_specs=[ospec, ospec, ospec_b, ospec_b],
        out_shape=[out_f, out_f, out_b, out_b],
        compiler_params=_params("parallel"),
        name="peer_topk",
    )(qt, keys)


def _peer_gate_tile(a, cnt_rows, p1_rows, r2_ref, p2_ref, lanes, *, heads, n_keys):
    ab = a.astype(BF16)
    gelu = 0.5 * ab * (1.0 + lax.erf(ab * (2.0 ** -0.5)))
    gate = jnp.zeros((n_keys, LANE), BF16)
    for h in range(heads):
        cnt = pltpu.bitcast(jnp.broadcast_to(cnt_rows[h][:, lanes], (n_keys // 2, LANE)), BF16)
        p1 = pltpu.bitcast(jnp.broadcast_to(p1_rows[h][:, lanes], (n_keys // 2, LANE)), BF16)
        r2 = pltpu.bitcast(r2_ref[h, :, lanes], BF16)
        p2 = pltpu.bitcast(p2_ref[h, :, lanes], BF16)
        gate = gate + jnp.where(r2 < cnt, p2 * p1, jnp.zeros_like(gate))
    return (gate * gelu).T


def _peer_dense_step(blk, c_new, c_old, xt_ref, u_ref, v_ref, cnt1_ref, p1_ref, r2_ref, p2_ref, out_ref,
                     *, heads, n_keys, groups):
    tm = xt_ref.shape[1]
    dc = out_ref.shape[1] // groups
    for r in range(groups):
        if c_new is not None:
            e1 = blk * groups + r
            rows = slice(r * n_keys, (r + 1) * n_keys)
            a = _dot(u_ref[rows, :], xt_ref[...])
            cnt_rows = [cnt1_ref[h, pl.ds(e1, 1), :] for h in range(heads)]
            p1_rows = [p1_ref[h, pl.ds(e1, 1), :] for h in range(heads)]
            for lt in range(tm // LANE):
                lanes = slice(lt * LANE, (lt + 1) * LANE)
                c_new[lanes, rows] = _peer_gate_tile(a[:, lanes], cnt_rows, p1_rows, r2_ref, p2_ref, lanes,
                                                     heads=heads, n_keys=n_keys)
        if c_old is not None:
            cols = slice(r * dc, (r + 1) * dc)
            out_ref[:, cols] += _dot(c_old[...], v_ref[:, cols])


def _peer_dense_kernel(xt_ref, u_ref, v_ref, cnt1_ref, p1_ref, r2_ref, p2_ref, res_ref, gain_ref, *rest,
                       heads, n_keys, groups, nblk, final):
    if final:
        out_ref, c0_ref, c1_ref = rest
    else:
        out_ref, norm_ref, c0_ref, c1_ref = rest
    g = pl.program_id(1)
    step = functools.partial(_peer_dense_step, g, xt_ref=xt_ref, u_ref=u_ref, v_ref=v_ref,
                             cnt1_ref=cnt1_ref, p1_ref=p1_ref, r2_ref=r2_ref, p2_ref=p2_ref, out_ref=out_ref,
                             heads=heads, n_keys=n_keys, groups=groups)
    bufs = (c0_ref, c1_ref)
    inner = jnp.logical_and(g > 0, g < nblk)

    @pl.when(g == 0)
    def _():
        out_ref[...] = res_ref[...]
        step(bufs[0], None)

    @pl.when(jnp.logical_and(inner, lax.rem(g, 2) == 0))
    def _():
        step(bufs[0], bufs[1])

    @pl.when(jnp.logical_and(inner, lax.rem(g, 2) == 1))
    def _():
        step(bufs[1], bufs[0])

    @pl.when(g == nblk)
    def _():
        step(None, bufs[(nblk - 1) % 2])
        x = out_ref[...]
        y = x * lax.rsqrt(jnp.mean(x * x, axis=-1, keepdims=True) + EPS) * gain_ref[...]
        if final:
            out_ref[...] = y
        else:
            norm_ref[...] = y.astype(norm_ref.dtype)


def peer_dense(xt, u, v, cnt1, p1, r2, p2, residual, post_gain, final):
    D, T = xt.shape
    heads, n_keys, _ = cnt1.shape
    E = u.shape[0]
    groups = 8
    eb = groups * n_keys
    nblk = E // eb
    tm = min(T, 512)
    gspec = pl.BlockSpec((heads, n_keys, tm), lambda i, g: (0, 0, i))
    gspec_b = pl.BlockSpec((heads, n_keys // 2, tm), lambda i, g: (0, 0, i))
    row_spec = pl.BlockSpec((tm, D), lambda i, g: (i, 0))
    out_specs = row_spec if final else [row_spec, row_spec]
    out_shape = jax.ShapeDtypeStruct((T, D), F32)
    if not final:
        out_shape = [out_shape, jax.ShapeDtypeStruct((T, D), BF16)]
    return pl.pallas_call(
        functools.partial(_peer_dense_kernel, heads=heads, n_keys=n_keys, groups=groups, nblk=nblk, final=final),
        grid=(T // tm, nblk + 1),
        in_specs=[pl.BlockSpec((D, tm), lambda i, g: (0, i)),
                  pl.BlockSpec((eb, D), lambda i, g: (jnp.minimum(g, nblk - 1), 0)),
                  pl.BlockSpec((eb, D), lambda i, g: (jnp.maximum(g - 1, 0), 0)),
                  gspec, gspec, gspec_b, gspec_b,
                  pl.BlockSpec((tm, D), lambda i, g: (i, 0), pipeline_mode=pl.Buffered(1)),
                  pl.BlockSpec((1, D), lambda i, g: (0, 0))],
        out_specs=out_specs,
        out_shape=out_shape,
        scratch_shapes=[pltpu.VMEM((tm, eb), BF16), pltpu.VMEM((tm, eb), BF16)],
        compiler_params=_params("parallel", "arbitrary"),
        name="peer_dense",
    )(xt, u, v, cnt1, p1, r2, p2, residual, post_gain.reshape(1, D).astype(F32))


def _cast_kernel(x_ref, o_ref):
    o_ref[...] = x_ref[0].astype(o_ref.dtype)


def cast_table(tabs, layer, dtype):
    _, rows, cols = tabs.shape
    tr = min(rows, 1024)
    return pl.pallas_call(
        _cast_kernel,
        grid=(rows // tr,),
        in_specs=[pl.BlockSpec((1, tr, cols), lambda i: (layer, i, 0))],
        out_specs=pl.BlockSpec((tr, cols), lambda i: (i, 0)),
        out_shape=jax.ShapeDtypeStruct((rows, cols), dtype),
        compiler_params=_params("parallel"),
        name="cast_table",
    )(tabs)


def peer_layer(x, ht, w_q, sub_keys, u_tabs, v_tabs, layer, post_gain, final):
    qt = matmul(w_q.T.astype(BF16), ht, BF16, tn=512, name="peer_q")
    cnt1, p1, r2, p2 = peer_topk(qt, sub_keys.astype(BF16))
    return peer_dense(ht, cast_table(u_tabs, layer, BF16), cast_table(v_tabs, layer, BF16), cnt1, p1, r2, p2, x,
                      post_gain, final)


def _pad_cols(w, n):
    return jnp.pad(w, ((0, 0), (0, n - w.shape[1])))


def even_layer(x, h, batch, seq, w_in, b_igate, b_fgate, head_gain, w_pool, pool_scale, w_out, peer_gain):
    H = MLSTM_HEADS
    width = head_gain.shape[0]
    qk = width // 2
    pool_w = pool_scale.shape[0]
    n_main = 2 * qk + 2 * width
    w_cat = jnp.concatenate([w_in[:, :n_main], w_in[:, n_main + 2 * H:], w_in[:, n_main:n_main + 2 * H]], axis=1)
    tn = 3 * MXU_COLS
    z = matmul(h, _pad_cols(w_cat, -(-w_cat.shape[1] // tn) * tn).astype(BF16), F32, tn=tn, name="even_in")
    gate_bias = jnp.pad(jnp.concatenate([b_igate, b_fgate]), (0, LANE - 2 * H)).reshape(1, LANE).astype(F32)
    hm = mlstm_block(z, gate_bias, head_gain, batch, seq, 0, qk, 2 * qk, 2 * qk + width, n_main + pool_w)
    hp = pool_block(z, w_pool, pool_scale, batch, seq, n_main)
    return out_proj([hm, hp], w_out.astype(BF16), x, peer_gain)


def odd_layer(x, h, batch, seq, w_in, w_gate, b_gate, head_gain, w_out, peer_gain):
    rank, dkt = w_gate.shape
    dvt = head_gain.shape[0]
    n_main = 2 * dkt + 2 * dvt
    tn = 5 * MXU_COLS
    z = matmul(h, _pad_cols(w_in, -(-w_in.shape[1] // tn) * tn).astype(BF16), F32, tn=tn, name="odd_in")
    wg = jnp.pad(w_gate, ((0, LANE - rank), (0, 0))).astype(BF16)
    y = gla_block(z, wg, b_gate, head_gain, batch, seq, 0, dkt, 2 * dkt, 2 * dkt + dvt, n_main)
    return out_proj([y], w_out.astype(BF16), x, peer_gain)


def kernel(x, e_norm, e_w_in, e_b_igate, e_b_fgate, e_head_gain, e_w_pool, e_pool_scale, e_w_out, o_norm, o_w_in, o_w_gate, o_b_gate, o_head_gain, o_w_out, f_norm, f_w_q, f_sub_keys, f_u, f_v, final_norm):
    B, S, D = x.shape
    depth = f_norm.shape[0]

    def mixer_gain(layer):
        return e_norm[layer // 2] if layer % 2 == 0 else o_norm[layer // 2]

    xs = x.reshape(B * S, D)
    h = rmsnorm(xs, mixer_gain(0), BF16)
    for layer in range(depth):
        j = layer // 2
        if layer % 2 == 0:
            xs, ht = even_layer(xs, h, B, S, e_w_in[j], e_b_igate[j], e_b_fgate[j], e_head_gain[j],
                                e_w_pool[j], e_pool_scale[j], e_w_out[j], f_norm[layer])
        else:
            xs, ht = odd_layer(xs, h, B, S, o_w_in[j], o_w_gate[j], o_b_gate[j], o_head_gain[j], o_w_out[j],
                               f_norm[layer])
        if layer == depth - 1:
            return peer_layer(xs, ht, f_w_q[layer], f_sub_keys[layer], f_u, f_v, layer,
                              final_norm, True).reshape(B, S, D)
        xs, h = peer_layer(xs, ht, f_w_q[layer], f_sub_keys[layer], f_u, f_v, layer,
                           mixer_gain(layer + 1), False)
```

```python
import functools

import jax
import jax.numpy as jnp
from jax import lax
from jax.experimental import pallas as pl
from jax.experimental.pallas import tpu as pltpu

F32 = jnp.float32
BF16 = jnp.bfloat16
EPS = 1e-6
LANE = 128
SUBLANE = 8
MXU_COLS = 256
VMEM_LIMIT = 56 * 1024 * 1024

MLSTM_HEADS = 4
MLSTM_CHUNK = 64
POOL_WINDOWS = (2, 4, 8, 16)
POOL_HALO = 16
GLA_HEADS = 4
GLA_CHUNK = 32
GLA_TAU = 16.0
PEER_TOPK = 16
NEG_INF = float("-inf")
LOG2E = 1.4426950408889634

NT_DIMS = (((1,), (1,)), ((), ()))
TN_DIMS = (((0,), (0,)), ((), ()))


def _params(*sem):
    return pltpu.CompilerParams(dimension_semantics=sem, vmem_limit_bytes=VMEM_LIMIT)


def _log_sigmoid(x):
    return jnp.minimum(x, 0.0) - jnp.log1p(jnp.exp(-jnp.abs(x)))


def _sigmoid(x):
    return 1.0 / (1.0 + jnp.exp(-x))


def _dot(a, b, dims=None):
    if dims is None:
        return jnp.dot(a, b, preferred_element_type=F32)
    return lax.dot_general(a, b, dims, preferred_element_type=F32)


def _cumsum_rows(x, tri):
    hi = x.astype(BF16)
    lo = (x - hi.astype(F32)).astype(BF16)
    return _dot(tri, hi) + _dot(tri, lo)


def _tril(n, dtype):
    r = lax.broadcasted_iota(jnp.int32, (n, n), 0)
    c = lax.broadcasted_iota(jnp.int32, (n, n), 1)
    return (c <= r).astype(dtype)


def _rms_kernel(x_ref, g_ref, o_ref):
    x = x_ref[...]
    y = x * lax.rsqrt(jnp.mean(x * x, axis=-1, keepdims=True) + EPS) * g_ref[...]
    o_ref[...] = y.astype(o_ref.dtype)


def rmsnorm(x, g, out_dtype):
    T, D = x.shape
    tm = min(T, 512)
    return pl.pallas_call(
        _rms_kernel,
        grid=(T // tm,),
        in_specs=[pl.BlockSpec((tm, D), lambda i: (i, 0)), pl.BlockSpec((1, D), lambda i: (0, 0))],
        out_specs=pl.BlockSpec((tm, D), lambda i: (i, 0)),
        out_shape=jax.ShapeDtypeStruct((T, D), out_dtype),
        compiler_params=_params("parallel"),
        name="rmsnorm",
    )(x, g.reshape(1, D).astype(F32))


def _mm_kernel(a_ref, b_ref, o_ref):
    o_ref[...] = _dot(a_ref[...], b_ref[...]).astype(o_ref.dtype)


def matmul(a, b, out_dtype, tn, name="matmul"):
    M, K = a.shape
    N = b.shape[1]
    tm = min(M, 1024)
    tn = min(N, tn)
    return pl.pallas_call(
        _mm_kernel,
        grid=(M // tm, N // tn),
        in_specs=[pl.BlockSpec((tm, K), lambda i, j: (i, 0)), pl.BlockSpec((K, tn), lambda i, j: (0, j))],
        out_specs=pl.BlockSpec((tm, tn), lambda i, j: (i, j)),
        out_shape=jax.ShapeDtypeStruct((M, N), out_dtype),
        compiler_params=_params("parallel", "parallel"),
        name=name,
    )(a, b)


def _out_proj_kernel(*refs, n_parts):
    parts = refs[:n_parts]
    w_ref, res_ref, gain_ref, x_ref, nt_ref = refs[n_parts:]
    acc = res_ref[...]
    k0 = 0
    for p in parts:
        kp = p.shape[1]
        acc = acc + _dot(p[...], w_ref[k0:k0 + kp, :])
        k0 += kp
    x_ref[...] = acc
    y = acc * lax.rsqrt(jnp.mean(acc * acc, axis=-1, keepdims=True) + EPS) * gain_ref[...]
    nt_ref[...] = y.T.astype(nt_ref.dtype)


def out_proj(parts, w, residual, norm_gain):
    T, N = residual.shape
    K = w.shape[0]
    tm = min(T, 512)
    in_specs = [pl.BlockSpec((tm, p.shape[1]), lambda i: (i, 0)) for p in parts]
    in_specs += [pl.BlockSpec((K, N), lambda i: (0, 0), pipeline_mode=pl.Buffered(1)),
                 pl.BlockSpec((tm, N), lambda i: (i, 0)),
                 pl.BlockSpec((1, N), lambda i: (0, 0))]
    return pl.pallas_call(
        functools.partial(_out_proj_kernel, n_parts=len(parts)),
        grid=(T // tm,),
        in_specs=in_specs,
        out_specs=[pl.BlockSpec((tm, N), lambda i: (i, 0)), pl.BlockSpec((N, tm), lambda i: (0, i))],
        out_shape=[jax.ShapeDtypeStruct((T, N), F32), jax.ShapeDtypeStruct((N, T), BF16)],
        compiler_params=_params("parallel"),
        name="out_proj",
    )(*parts, w, residual, norm_gain.reshape(1, N).astype(F32))


def _mlstm_kernel(q_ref, k_ref, v_ref, o_ref, g_ref, gb_ref, gain_ref, out_ref, cn_ref, m_ref,
                  *, batch, heads, dk, dv, chunk, nchunks):
    L = chunk

    @pl.when(pl.program_id(0) == 0)
    def _():
        cn_ref[...] = jnp.zeros_like(cn_ref)
        m_ref[...] = jnp.zeros_like(m_ref)

    tri_b = _tril(L, BF16)
    causal = _tril(L, jnp.int32) > 0
    ones_col = (lax.broadcasted_iota(jnp.int32, (L, LANE), 1) == 0).astype(F32)
    kscale = dk ** -0.5

    chains = [(b, h) for b in range(batch) for h in range(heads)]
    n_chain = len(chains)

    def per_chain(fn):
        return jnp.stack([fn(b, h) for b, h in chains], axis=0)

    def chunk_body(c, carry):
        r0 = pl.multiple_of(c * L, L)
        rows = pl.ds(r0, L)
        g = [g_ref[b, rows, :] + gb_ref[...] for b in range(batch)]
        bc = [_cumsum_rows(_log_sigmoid(g[b]), tri_b) for b in range(batch)]
        g_t = [x.T for x in g]
        bc_t = [x.T for x in bc]
        i_col = per_chain(lambda b, h: g[b][:, h:h + 1])
        i_row = per_chain(lambda b, h: g_t[b][h:h + 1, :])
        b_col = per_chain(lambda b, h: bc[b][:, heads + h:heads + h + 1])
        b_row = per_chain(lambda b, h: bc_t[b][heads + h:heads + h + 1, :])
        m_prev = m_ref[:, 0:1, 0:1]

        dmat = jnp.where(causal, b_col - b_row + i_row, NEG_INF)
        inter = b_col + m_prev
        m_t = jnp.maximum(jnp.max(dmat, axis=-1, keepdims=True), inter)
        a_inter = jnp.exp(inter - m_t)
        decay = jnp.exp(dmat - m_t)
        b_last = b_col[:, L - 1:L, :]
        dec = b_last - b_col + i_col
        m_new = jnp.maximum(b_last + m_prev, jnp.max(dec, axis=1, keepdims=True))
        wk = jnp.exp(dec - m_new)
        sc = jnp.exp(b_last + m_prev - m_new)

        qb = [q_ref[b, rows, h * dk:(h + 1) * dk].astype(BF16) for b, h in chains]
        kb = [(k_ref[b, rows, h * dk:(h + 1) * dk] * kscale).astype(BF16) for b, h in chains]
        v_aug = per_chain(lambda b, h: jnp.concatenate([v_ref[b, rows, h * dv:(h + 1) * dv], ones_col], axis=1))
        v_aug_b = v_aug.astype(BF16)
        s = (jnp.stack([_dot(qb[i], kb[i], NT_DIMS) for i in range(n_chain)], axis=0) * decay).astype(BF16)
        intra = jnp.stack([_dot(s[i], v_aug_b[i]) for i in range(n_chain)], axis=0)
        inter_out = jnp.stack([_dot(qb[i], cn_ref[i].astype(BF16)) for i in range(n_chain)], axis=0)
        hn = intra + a_inter * inter_out

        wv = (wk * v_aug).astype(BF16)
        cn_ref[...] = sc * cn_ref[...] + jnp.stack([_dot(kb[i], wv[i], TN_DIMS) for i in range(n_chain)], axis=0)
        m_ref[...] = jnp.broadcast_to(m_new, m_ref.shape)

        hh = hn[:, :, :dv] / jnp.maximum(jnp.abs(hn[:, :, dv:dv + 1]), jnp.exp(-m_t))
        gain = per_chain(lambda b, h: gain_ref[:, h * dv:(h + 1) * dv])
        o_gate = _sigmoid(per_chain(lambda b, h: o_ref[b, rows, h * dv:(h + 1) * dv]))
        y = (hh * lax.rsqrt(jnp.mean(hh * hh, axis=-1, keepdims=True) + EPS) * gain * o_gate).astype(out_ref.dtype)
        for i, (b, h) in enumerate(chains):
            out_ref[b, rows, h * dv:(h + 1) * dv] = y[i]
        return carry

    lax.fori_loop(0, nchunks, chunk_body, 0)


def mlstm_block(z, gate_bias, head_gain, batch, seq, q_off, k_off, v_off, o_off, g_off):
    T, cols = z.shape
    H, L = MLSTM_HEADS, MLSTM_CHUNK
    width = head_gain.shape[0]
    dv = width // H
    dk = dv // 2
    ts = min(seq, 256)
    z3 = z.reshape(batch, seq, cols)

    def col(off, w):
        assert off % w == 0
        return pl.BlockSpec((batch, ts, w), lambda s: (0, s, off // w))

    n_state = batch * H
    out = pl.pallas_call(
        functools.partial(_mlstm_kernel, batch=batch, heads=H, dk=dk, dv=dv, chunk=L, nchunks=ts // L),
        grid=(seq // ts,),
        in_specs=[col(q_off, H * dk), col(k_off, H * dk), col(v_off, width), col(o_off, width), col(g_off, LANE),
                  pl.BlockSpec((1, LANE), lambda s: (0, 0)), pl.BlockSpec((1, width), lambda s: (0, 0))],
        out_specs=pl.BlockSpec((batch, ts, width), lambda s: (0, s, 0)),
        out_shape=jax.ShapeDtypeStruct((batch, seq, width), BF16),
        scratch_shapes=[pltpu.VMEM((n_state, dk, dv + LANE), F32),
                        pltpu.VMEM((n_state, SUBLANE, LANE), F32)],
        compiler_params=_params("arbitrary"),
        name="mlstm",
    )(z3, z3, z3, z3, z3, gate_bias, head_gain.reshape(1, width).astype(F32))
    return out.reshape(T, width)


def _pool_kernel(p_ref, w_ref, s_ref, out_ref, halo_ref, *, group, ts):
    si = pl.program_id(1)

    @pl.when(si == 0)
    def _():
        halo_ref[...] = jnp.zeros_like(halo_ref)

    p = p_ref[...]
    buf = jnp.concatenate([halo_ref[...], p], axis=0)
    t_abs = si * ts + lax.broadcasted_iota(jnp.int32, (ts, 1), 0)
    win = buf
    span = 1
    for g, w in enumerate(POOL_WINDOWS):
        while span < w:
            win = win + pltpu.roll(win, span, 0)
            span *= 2
        sl = slice(g * group, (g + 1) * group)
        cnt = jnp.minimum(t_abs + 1, w).astype(F32)
        mixed = win[POOL_HALO:, sl] / cnt - p[:, sl]
        y = _dot(mixed.astype(BF16), w_ref[g]) * s_ref[:, sl]
        out_ref[:, sl] = y.astype(out_ref.dtype)
    halo_ref[...] = p[ts - POOL_HALO:, :]


def pool_block(z, w_pool, pool_scale, batch, seq, p_off):
    T = z.shape[0]
    ng, group, _ = w_pool.shape
    width = ng * group
    ts = min(seq, 512)
    nblk = seq // ts
    assert p_off % width == 0
    return pl.pallas_call(
        functools.partial(_pool_kernel, group=group, ts=ts),
        grid=(batch, nblk),
        in_specs=[pl.BlockSpec((ts, width), lambda b, s: (b * nblk + s, p_off // width)),
                  pl.BlockSpec((ng, group, group), lambda b, s: (0, 0, 0)),
                  pl.BlockSpec((1, width), lambda b, s: (0, 0))],
        out_specs=pl.BlockSpec((ts, width), lambda b, s: (b * nblk + s, 0)),
        out_shape=jax.ShapeDtypeStruct((T, width), BF16),
        scratch_shapes=[pltpu.VMEM((POOL_HALO, width), F32)],
        compiler_params=_params("parallel", "arbitrary"),
        name="pool",
    )(z, w_pool.astype(BF16), pool_scale.reshape(1, width).astype(F32))


def _gla_kernel(q_ref, k_ref, v_ref, r_ref, gl_ref, wg_ref, bg_ref, gain_ref, out_ref, st_ref,
                *, batch, heads, dk, dv, chunk, nchunks):
    L = chunk
    SB = SUBLANE

    @pl.when(pl.program_id(0) == 0)
    def _():
        st_ref[...] = jnp.zeros_like(st_ref)

    tri_b = _tril(L, BF16)
    row_id = lax.broadcasted_iota(jnp.int32, (L, 1), 0)
    row_sub = lax.broadcasted_iota(jnp.int32, (SB, 1), 0)
    lane_id = lax.broadcasted_iota(jnp.int32, (SB, L), 1)
    qscale = dk ** -0.5
    chains = [(b, h) for b in range(batch) for h in range(heads)]
    n_chain = len(chains)

    def per_chain(fn):
        return jnp.stack([fn(b, h) for b, h in chains], axis=0)

    def chunk_body(c, carry):
        r0 = pl.multiple_of(c * L, L)
        rows = pl.ds(r0, L)
        b_seq = []
        for b in range(batch):
            la = _log_sigmoid(_dot(gl_ref[b, rows, :].astype(BF16), wg_ref[...]) + bg_ref[...]) / GLA_TAU
            b_seq.append(_cumsum_rows(la, tri_b))
        bb = per_chain(lambda b, h: b_seq[b][:, h * dk:(h + 1) * dk]) * LOG2E
        q = per_chain(lambda b, h: q_ref[b, rows, h * dk:(h + 1) * dk]) * qscale
        k = per_chain(lambda b, h: k_ref[b, rows, h * dk:(h + 1) * dk])

        att_sub = [jnp.zeros((n_chain, SB, L), F32) for _ in range(L // SB)]
        for s in range(L):
            blk = s // SB
            h0 = blk * SB
            rel = jnp.where(row_sub >= s - h0, bb[:, h0:h0 + SB, :] - bb[:, s:s + 1, :], NEG_INF)
            col = jnp.sum(q[:, h0:h0 + SB, :] * k[:, s:s + 1, :] * jnp.exp2(rel), axis=-1, keepdims=True)
            att_sub[blk] = jnp.where(lane_id == s, col, att_sub[blk])
        att = jnp.concatenate(att_sub, axis=1)
        for j in range(L // SB - 1):
            end = (j + 1) * SB
            anchor = bb[:, end - 1:end, :]
            qd = (q * jnp.exp2(jnp.where(row_id >= end, bb - anchor, NEG_INF))).astype(BF16)
            in_blk = jnp.logical_and(row_id >= end - SB, row_id < end)
            kd = (k * jnp.exp2(jnp.where(in_blk, anchor - bb, NEG_INF))).astype(BF16)
            att = att + jnp.stack([_dot(qd[i], kd[i], NT_DIMS) for i in range(n_chain)], axis=0)
        att = att.astype(BF16)

        b_last = bb[:, L - 1:L, :]
        q_in = (q * jnp.exp2(bb)).astype(BF16)
        k_out = (k * jnp.exp2(b_last - bb)).astype(BF16)
        vb = [v_ref[b, rows, h * dv:(h + 1) * dv].astype(BF16) for b, h in chains]
        o = jnp.stack([_dot(att[i], vb[i]) + _dot(q_in[i], st_ref[i].astype(BF16), NT_DIMS) for i in range(n_chain)], axis=0)
        st_ref[...] = jnp.exp2(b_last) * st_ref[...] + jnp.stack([_dot(vb[i], k_out[i], TN_DIMS) for i in range(n_chain)], axis=0)

        gain = per_chain(lambda b, h: gain_ref[:, h * dv:(h + 1) * dv])
        r = per_chain(lambda b, h: r_ref[b, rows, h * dv:(h + 1) * dv])
        y = o * lax.rsqrt(jnp.mean(o * o, axis=-1, keepdims=True) + EPS) * gain * (r * _sigmoid(r))
        y = y.astype(out_ref.dtype)
        for i, (b, h) in enumerate(chains):
            out_ref[b, rows, h * dv:(h + 1) * dv] = y[i]
        return carry

    lax.fori_loop(0, nchunks, chunk_body, 0)


def gla_block(z, w_gate, b_gate, head_gain, batch, seq, q_off, k_off, v_off, r_off, g_off):
    T, cols = z.shape
    H, L = GLA_HEADS, GLA_CHUNK
    dkt = w_gate.shape[1]
    dvt = head_gain.shape[0]
    dk, dv = dkt // H, dvt // H
    ts = min(seq, 128)
    z3 = z.reshape(batch, seq, cols)

    def col(off, w):
        assert off % w == 0
        return pl.BlockSpec((batch, ts, w), lambda s: (0, s, off // w))

    out = pl.pallas_call(
        functools.partial(_gla_kernel, batch=batch, heads=H, dk=dk, dv=dv, chunk=L, nchunks=ts // L),
        grid=(seq // ts,),
        in_specs=[col(q_off, dkt), col(k_off, dkt), col(v_off, dvt), col(r_off, dvt), col(g_off, LANE),
                  pl.BlockSpec((LANE, dkt), lambda s: (0, 0)),
                  pl.BlockSpec((1, dkt), lambda s: (0, 0)),
                  pl.BlockSpec((1, dvt), lambda s: (0, 0))],
        out_specs=pl.BlockSpec((batch, ts, dvt), lambda s: (0, s, 0)),
        out_shape=jax.ShapeDtypeStruct((batch, seq, dvt), BF16),
        scratch_shapes=[pltpu.VMEM((batch * H, dv, dk), F32)],
        compiler_params=_params("arbitrary"),
        name="gla",
    )(z3, z3, z3, z3, z3, w_gate, b_gate.reshape(1, dkt).astype(F32), head_gain.reshape(1, dvt).astype(F32))
    return out.reshape(T, dvt)


def _extract_topk(vals, k, with_rank):
    rank = jnp.full(vals.shape, float(k), F32) if with_rank else None
    tops = []
    for i in range(k):
        mx = jnp.max(vals, axis=0, keepdims=True)
        hit = vals == mx
        if with_rank:
            rank = jnp.where(hit, float(i), rank)
        vals = jnp.where(hit, NEG_INF, vals)
        tops.append(mx)
    return jnp.concatenate(tops, axis=0), rank


def _count_partners(s1, sv2, tau, k):
    def test(row):
        return s1 + row >= tau

    def pick(bits, lo, step):
        if not bits:
            return sv2[lo:lo + 1, :]
        return jnp.where(bits[0], pick(bits[1:], lo + step, step // 2), pick(bits[1:], lo, step // 2))

    bits = []
    step = k // 2
    while step >= 1:
        bits.append(test(pick(bits, step - 1, k // 2)))
        step //= 2
    cnt = jnp.zeros_like(s1)
    weight = k // 2
    for bit in bits:
        cnt = cnt + jnp.where(bit, float(weight), 0.0)
        weight //= 2
    return cnt + jnp.where(test(sv2[k - 1:k, :]), 1.0, 0.0)


def _dup_bf16(x):
    hi = pltpu.bitcast(x.astype(BF16).astype(F32), jnp.uint32)
    return hi | lax.shift_right_logical(hi, jnp.uint32(16))


def _peer_topk_kernel(qt_ref, keys_ref, cnt1_ref, p1_ref, r2_ref, p2_ref, *, heads, half, topk):
    K = topk
    for h in range(heads):
        s1 = _dot(keys_ref[h, 0], qt_ref[(2 * h) * half:(2 * h + 1) * half, :])
        s2 = _dot(keys_ref[h, 1], qt_ref[(2 * h + 1) * half:(2 * h + 2) * half, :])
        sv1, _ = _extract_topk(s1, K, False)
        sv2, rank2 = _extract_topk(s2, K, True)
        cand = []
        a = 0
        while K // (a + 1) > 1:
            nb = K // (a + 1)
            rows = -(-nb // SUBLANE) * SUBLANE
            c = sv1[a:a + 1, :] + sv2[:rows, :]
            if nb < rows:
                c = jnp.where(lax.broadcasted_iota(jnp.int32, (rows, 1), 0) < nb, c, NEG_INF)
            cand.append(c)
            a += 1
        n_single = a
        cand.append(sv1[n_single:, :] + sv2[0:1, :])
        work = jnp.concatenate(cand, axis=0)
        tau = None
        for _ in range(K):
            tau = jnp.max(work, axis=0, keepdims=True)
            work = jnp.where(work == tau, NEG_INF, work)
        cmax = sv1[0:1, :] + sv2[0:1, :]
        zsum = jnp.zeros_like(cmax)
        for c in cand:
            zsum = zsum + jnp.sum(jnp.where(c >= tau, jnp.exp(c - cmax), 0.0), axis=0, keepdims=True)
        cnt1_ref[h] = _dup_bf16(_count_partners(s1, sv2, tau, K))
        p1_ref[h] = _dup_bf16(jnp.exp(s1 - sv1[0:1, :]) / zsum)
        r2_ref[h] = pltpu.bitcast(rank2.astype(BF16), jnp.uint32)
        p2_ref[h] = pltpu.bitcast(jnp.exp(s2 - sv2[0:1, :]).astype(BF16), jnp.uint32)


def peer_topk(qt, keys):
    heads, _, n_keys, half = keys.shape
    T = qt.shape[1]
    tt = min(T, 256)
    out_f = jax.ShapeDtypeStruct((heads, n_keys, T), jnp.uint32)
    out_b = jax.ShapeDtypeStruct((heads, n_keys // 2, T), jnp.uint32)
    ospec = pl.BlockSpec((heads, n_keys, tt), lambda i: (0, 0, i))
    ospec_b = pl.BlockSpec((heads, n_keys // 2, tt), lambda i: (0, 0, i))
    return pl.pallas_call(
        functools.partial(_peer_topk_kernel, heads=heads, half=half, topk=PEER_TOPK),
        grid=(T // tt,),
        in_specs=[pl.BlockSpec((heads * 2 * half, tt), lambda i: (0, i)),
                  pl.BlockSpec((heads, 2, n_keys, half), lambda i: (0, 0, 0, 0))],
        out_specs=[ospec, ospec, ospec_b, ospec_b],
        out_shape=[out_f, out_f, out_b, out_b],
        compiler_params=_params("parallel"),
        name="peer_topk",
    )(qt, keys)


def _peer_gate_tile(a, cnt_rows, p1_rows, r2_ref, p2_ref, lanes, *, heads, n_keys):
    ab = a.astype(BF16)
    gelu = 0.5 * ab * (1.0 + lax.erf(ab * (2.0 ** -0.5)))
    gate = jnp.zeros((n_keys, LANE), BF16)
    for h in range(heads):
        cnt = pltpu.bitcast(jnp.broadcast_to(cnt_rows[h][:, lanes], (n_keys // 2, LANE)), BF16)
        p1 = pltpu.bitcast(jnp.broadcast_to(p1_rows[h][:, lanes], (n_keys // 2, LANE)), BF16)
        r2 = pltpu.bitcast(r2_ref[h, :, lanes], BF16)
        p2 = pltpu.bitcast(p2_ref[h, :, lanes], BF16)
        gate = gate + jnp.where(r2 < cnt, p2 * p1, jnp.zeros_like(gate))
    return (gate * gelu).T


def _peer_dense_step(blk, c_new, c_old, xt_ref, u_ref, v_ref, cnt1_ref, p1_ref, r2_ref, p2_ref, out_ref,
                     *, heads, n_keys, groups):
    tm = xt_ref.shape[1]
    dc = out_ref.shape[1] // groups
    for r in range(groups):
        if c_new is not None:
            e1 = blk * groups + r
            rows = slice(r * n_keys, (r + 1) * n_keys)
            a = _dot(u_ref[rows, :], xt_ref[...])
            cnt_rows = [cnt1_ref[h, pl.ds(e1, 1), :] for h in range(heads)]
            p1_rows = [p1_ref[h, pl.ds(e1, 1), :] for h in range(heads)]
            for lt in range(tm // LANE):
                lanes = slice(lt * LANE, (lt + 1) * LANE)
                c_new[lanes, rows] = _peer_gate_tile(a[:, lanes], cnt_rows, p1_rows, r2_ref, p2_ref, lanes,
                                                     heads=heads, n_keys=n_keys)
        if c_old is not None:
            cols = slice(r * dc, (r + 1) * dc)
            out_ref[:, cols] += _dot(c_old[...], v_ref[:, cols])


def _peer_dense_kernel(xt_ref, u_ref, v_ref, cnt1_ref, p1_ref, r2_ref, p2_ref, res_ref, gain_ref, *rest,
                       heads, n_keys, groups, nblk, final):
    if final:
        out_ref, c0_ref, c1_ref = rest
    else:
        out_ref, norm_ref, c0_ref, c1_ref = rest
    g = pl.program_id(1)
    step = functools.partial(_peer_dense_step, g, xt_ref=xt_ref, u_ref=u_ref, v_ref=v_ref,
                             cnt1_ref=cnt1_ref, p1_ref=p1_ref, r2_ref=r2_ref, p2_ref=p2_ref, out_ref=out_ref,
                             heads=heads, n_keys=n_keys, groups=groups)
    bufs = (c0_ref, c1_ref)
    inner = jnp.logical_and(g > 0, g < nblk)

    @pl.when(g == 0)
    def _():
        out_ref[...] = res_ref[...]
        step(bufs[0], None)

    @pl.when(jnp.logical_and(inner, lax.rem(g, 2) == 0))
    def _():
        step(bufs[0], bufs[1])

    @pl.when(jnp.logical_and(inner, lax.rem(g, 2) == 1))
    def _():
        step(bufs[1], bufs[0])

    @pl.when(g == nblk)
    def _():
        step(None, bufs[(nblk - 1) % 2])
        x = out_ref[...]
        y = x * lax.rsqrt(jnp.mean(x * x, axis=-1, keepdims=True) + EPS) * gain_ref[...]
        if final:
            out_ref[...] = y
        else:
            norm_ref[...] = y.astype(norm_ref.dtype)


def peer_dense(xt, u, v, cnt1, p1, r2, p2, residual, post_gain, final):
    D, T = xt.shape
    heads, n_keys, _ = cnt1.shape
    E = u.shape[0]
    groups = 8
    eb = groups * n_keys
    nblk = E // eb
    tm = min(T, 512)
    gspec = pl.BlockSpec((heads, n_keys, tm), lambda i, g: (0, 0, i))
    gspec_b = pl.BlockSpec((heads, n_keys // 2, tm), lambda i, g: (0, 0, i))
    row_spec = pl.BlockSpec((tm, D), lambda i, g: (i, 0))
    out_specs = row_spec if final else [row_spec, row_spec]
    out_shape = jax.ShapeDtypeStruct((T, D), F32)
    if not final:
        out_shape = [out_shape, jax.ShapeDtypeStruct((T, D), BF16)]
    return pl.pallas_call(
        functools.partial(_peer_dense_kernel, heads=heads, n_keys=n_keys, groups=groups, nblk=nblk, final=final),
        grid=(T // tm, nblk + 1),
        in_specs=[pl.BlockSpec((D, tm), lambda i, g: (0, i)),
                  pl.BlockSpec((eb, D), lambda i, g: (jnp.minimum(g, nblk - 1), 0)),
                  pl.BlockSpec((eb, D), lambda i, g: (jnp.maximum(g - 1, 0), 0)),
                  gspec, gspec, gspec_b, gspec_b,
                  pl.BlockSpec((tm, D), lambda i, g: (i, 0), pipeline_mode=pl.Buffered(1)),
                  pl.BlockSpec((1, D), lambda i, g: (0, 0))],
        out_specs=out_specs,
        out_shape=out_shape,
        scratch_shapes=[pltpu.VMEM((tm, eb), BF16), pltpu.VMEM((tm, eb), BF16)],
        compiler_params=_params("parallel", "arbitrary"),
        name="peer_dense",
    )(xt, u, v, cnt1, p1, r2, p2, residual, post_gain.reshape(1, D).astype(F32))


def _cast_kernel(x_ref, o_ref):
    o_ref[...] = x_ref[0].astype(o_ref.dtype)


def cast_table(tabs, layer, dtype):
    _, rows, cols = tabs.shape
    tr = min(rows, 1024)
    return pl.pallas_call(
        _cast_kernel,
        grid=(rows // tr,),
        in_specs=[pl.BlockSpec((1, tr, cols), lambda i: (layer, i, 0))],
        out_specs=pl.BlockSpec((tr, cols), lambda i: (i, 0)),
        out_shape=jax.ShapeDtypeStruct((rows, cols), dtype),
        compiler_params=_params("parallel"),
        name="cast_table",
    )(tabs)


def peer_layer(x, ht, w_q, sub_keys, u_tabs, v_tabs, layer, post_gain, final):
    qt = matmul(w_q.T.astype(BF16), ht, BF16, tn=512, name="peer_q")
    cnt1, p1, r2, p2 = peer_topk(qt, sub_keys.astype(BF16))
    return peer_dense(ht, cast_table(u_tabs, layer, BF16), cast_table(v_tabs, layer, BF16), cnt1, p1, r2, p2, x,
                      post_gain, final)


def _pad_cols(w, n):
    return jnp.pad(w, ((0, 0), (0, n - w.shape[1])))


def even_layer(x, h, batch, seq, w_in, b_igate, b_fgate, head_gain, w_pool, pool_scale, w_out, peer_gain):
    H = MLSTM_HEADS
    width = head_gain.shape[0]
    qk = width // 2
    pool_w = pool_scale.shape[0]
    n_main = 2 * qk + 2 * width
    w_cat = jnp.concatenate([w_in[:, :n_main], w_in[:, n_main + 2 * H:], w_in[:, n_main:n_main + 2 * H]], axis=1)
    tn = 3 * MXU_COLS
    z = matmul(h, _pad_cols(w_cat, -(-w_cat.shape[1] // tn) * tn).astype(BF16), F32, tn=tn, name="even_in")
    gate_bias = jnp.pad(jnp.concatenate([b_igate, b_fgate]), (0, LANE - 2 * H)).reshape(1, LANE).astype(F32)
    hm = mlstm_block(z, gate_bias, head_gain, batch, seq, 0, qk, 2 * qk, 2 * qk + width, n_main + pool_w)
    hp = pool_block(z, w_pool, pool_scale, batch, seq, n_main)
    return out_proj([hm, hp], w_out.astype(BF16), x, peer_gain)


def odd_layer(x, h, batch, seq, w_in, w_gate, b_gate, head_gain, w_out, peer_gain):
    rank, dkt = w_gate.shape
    dvt = head_gain.shape[0]
    n_main = 2 * dkt + 2 * dvt
    tn = 5 * MXU_COLS
    z = matmul(h, _pad_cols(w_in, -(-w_in.shape[1] // tn) * tn).astype(BF16), F32, tn=tn, name="odd_in")
    wg = jnp.pad(w_gate, ((0, LANE - rank), (0, 0))).astype(BF16)
    y = gla_block(z, wg, b_gate, head_gain, batch, seq, 0, dkt, 2 * dkt, 2 * dkt + dvt, n_main)
    return out_proj([y], w_out.astype(BF16), x, peer_gain)


def kernel(x, e_norm, e_w_in, e_b_igate, e_b_fgate, e_head_gain, e_w_pool, e_pool_scale, e_w_out, o_norm, o_w_in, o_w_gate, o_b_gate, o_head_gain, o_w_out, f_norm, f_w_q, f_sub_keys, f_u, f_v, final_norm):
    B, S, D = x.shape
    depth = f_norm.shape[0]

    def mixer_gain(layer):
        return e_norm[layer // 2] if layer % 2 == 0 else o_norm[layer // 2]

    xs = x.reshape(B * S, D)
    h = rmsnorm(xs, mixer_gain(0), BF16)
    for layer in range(depth):
        j = layer // 2
        if layer % 2 == 0:
            xs, ht = even_layer(xs, h, B, S, e_w_in[j], e_b_igate[j], e_b_fgate[j], e_head_gain[j],
                                e_w_pool[j], e_pool_scale[j], e_w_out[j], f_norm[layer])
        else:
            xs, ht = odd_layer(xs, h, B, S, o_w_in[j], o_w_gate[j], o_b_gate[j], o_head_gain[j], o_w_out[j],
                               f_norm[layer])
        if layer == depth - 1:
            return peer_layer(xs, ht, f_w_q[layer], f_sub_keys[layer], f_u, f_v, layer,
                              final_norm, True).reshape(B, S, D)
        xs, h = peer_layer(xs, ht, f_w_q[layer], f_sub_keys[layer], f_u, f_v, layer,
                           mixer_gain(layer + 1), False)
```
